```python
import jax, jax.numpy as jnp
from jax import lax
import numpy as np

D_MODEL = 1024
BATCH = 8
SEQ = 2048
DEPTH = 4
DEC_BATCH = 128
DEC_SEQ = 1
PAST_LEN = 16384
PAGE_SIZE = 128

LRU_WIDTH = D_MODEL
LRU_HEADS = 16
LRU_BLOCK = LRU_WIDTH // LRU_HEADS
CONV_A_WIDTH = 4
LRU_C = 8.0
SC_WIDTH = D_MODEL
CONV_B_WIDTH = 3
PLE_DIM = 256
D_FF = 3 * D_MODEL
N_EXPERTS = 8
TOP_K = 2
D_FF_EXPERT = 3 * D_MODEL // 2
N_DENSE = (DEPTH + 1) // 2
N_MOE = DEPTH // 2
EPS = 1e-6
IN_COLS = 2 * LRU_WIDTH + 3 * SC_WIDTH + 2 * D_MODEL
SPLIT_IDX = (LRU_WIDTH, 2 * LRU_WIDTH, 2 * LRU_WIDTH + SC_WIDTH,
             2 * LRU_WIDTH + 2 * SC_WIDTH, 2 * LRU_WIDTH + 3 * SC_WIDTH,
             2 * LRU_WIDTH + 3 * SC_WIDTH + D_MODEL)

kernel_name = "hawk_shortconv_parallel_hybrid_step"


def rms_norm(x, g):
    xf = x.astype(jnp.float32)
    var = jnp.mean(xf * xf, axis=-1, keepdims=True)
    return (xf * lax.rsqrt(var + EPS) * g.astype(jnp.float32)).astype(x.dtype)


def causal_dwconv(x, buf, w):
    k = w.shape[0]
    t = x.shape[1]
    xp = jnp.concatenate([buf.astype(x.dtype), x], axis=1)
    y = xp[:, 0:t] * w[0]
    for j in range(1, k):
        y = y + xp[:, j:j + t] * w[j]
    return y, xp[:, t:]


def rg_lru(x, wx, bx, wa, ba, lam, h0):
    b_, t, w = x.shape
    xh = x.reshape(b_, t, LRU_HEADS, LRU_BLOCK)
    gate_x = jax.nn.sigmoid((jnp.einsum('bthi,hij->bthj', xh, wx).reshape(b_, t, w) + bx).astype(jnp.float32))
    gate_a = jax.nn.sigmoid((jnp.einsum('bthi,hij->bthj', xh, wa).reshape(b_, t, w) + ba).astype(jnp.float32))
    log_a = LRU_C * gate_a * jax.nn.log_sigmoid(lam.astype(jnp.float32))
    a = jnp.exp(log_a)
    mult = jnp.sqrt(-jnp.expm1(2.0 * log_a))
    b = mult * gate_x * x.astype(jnp.float32)
    b = b.at[:, 0].add(a[:, 0] * h0.astype(jnp.float32))

    def combine(left, right):
        al, bl = left
        ar, br = right
        return al * ar, ar * bl + br

    _, h = lax.associative_scan(combine, (a, b), axis=1)
    return h.astype(x.dtype), h[:, -1]


def swiglu(x, w_gu, w_dn):
    g, u = jnp.split(x @ w_gu, 2, axis=-1)
    return (jax.nn.silu(g) * u) @ w_dn


def moe_swiglu(x, router, w_gu, w_dn):
    logits = jnp.einsum('btd,de->bte', x, router).astype(jnp.float32)
    top_v, top_i = lax.top_k(logits, TOP_K)
    top_w = jax.nn.softmax(top_v, axis=-1)
    comb = jnp.einsum('btk,btke->bte', top_w,
                      jax.nn.one_hot(top_i, N_EXPERTS, dtype=jnp.float32)).astype(x.dtype)
    out = jnp.zeros_like(x)
    for e in range(N_EXPERTS):
        out = out + comb[..., e:e + 1] * swiglu(x, w_gu[e], w_dn[e])
    return out


def setup_inputs(seed: int = 0) -> dict:
    key = jax.random.key(seed)
    ks = iter(jax.random.split(key, 40))
    f32 = jnp.float32

    def nrm(shape, scale):
        return jax.random.normal(next(ks), shape, f32) * scale

    def gain(shape):
        return 1.0 + 0.05 * jax.random.normal(next(ks), shape, f32)

    out_scale = (2.0 * DEPTH) ** -0.5
    a0 = jax.random.uniform(next(ks), (DEPTH, LRU_WIDTH), f32, 0.9, 0.999)
    s = a0 ** (1.0 / LRU_C)
    lru_lambda = jnp.log(s) - jnp.log1p(-s)
    return {
        "x_prompt": nrm((BATCH, SEQ, D_MODEL), 1.0),
        "x_sample": nrm((DEC_BATCH, DEC_SEQ, D_MODEL), 1.0),
        "state_lru_h": nrm((DEPTH, DEC_BATCH, LRU_WIDTH), 0.5),
        "state_conv_a": nrm((DEPTH, DEC_BATCH, CONV_A_WIDTH - 1, LRU_WIDTH), 1.0),
        "state_conv_b": nrm((DEPTH, DEC_BATCH, CONV_B_WIDTH - 1, SC_WIDTH), 1.0),
        "p_prompt": nrm((DEPTH, BATCH, SEQ, PLE_DIM), 1.0),
        "p_sample": nrm((DEPTH, DEC_BATCH, DEC_SEQ, PLE_DIM), 1.0),
        "norm_mix": gain((DEPTH, D_MODEL)),
        "w_in": nrm((DEPTH, D_MODEL, IN_COLS), D_MODEL ** -0.5),
        "conv_a_w": nrm((DEPTH, CONV_A_WIDTH, LRU_WIDTH), CONV_A_WIDTH ** -0.5),
        "conv_a_b": nrm((DEPTH, LRU_WIDTH), 0.02),
        "lru_wx": nrm((DEPTH, LRU_HEADS, LRU_BLOCK, LRU_BLOCK), LRU_BLOCK ** -0.5),
        "lru_bx": nrm((DEPTH, LRU_WIDTH), 0.02),
        "lru_wa": nrm((DEPTH, LRU_HEADS, LRU_BLOCK, LRU_BLOCK), LRU_BLOCK ** -0.5),
        "lru_ba": nrm((DEPTH, LRU_WIDTH), 0.02),
        "lru_lambda": lru_lambda,
        "w_a_out": nrm((DEPTH, LRU_WIDTH, D_MODEL), LRU_WIDTH ** -0.5),
        "conv_b_w": nrm((DEPTH, CONV_B_WIDTH, SC_WIDTH), CONV_B_WIDTH ** -0.5),
        "w_b_out": nrm((DEPTH, SC_WIDTH, D_MODEL), SC_WIDTH ** -0.5),
        "w_o": nrm((DEPTH, D_MODEL, D_MODEL), D_MODEL ** -0.5 * out_scale),
        "norm_ffn": gain((DEPTH, D_MODEL)),
        "ffn_w_gu": nrm((N_DENSE, D_MODEL, 2 * D_FF), D_MODEL ** -0.5),
        "ffn_w_dn": nrm((N_DENSE, D_FF, D_MODEL), D_FF ** -0.5 * out_scale),
        "router": nrm((N_MOE, D_MODEL, N_EXPERTS), D_MODEL ** -0.5),
        "moe_w_gu": nrm((N_MOE, N_EXPERTS, D_MODEL, 2 * D_FF_EXPERT), D_MODEL ** -0.5),
        "moe_w_dn": nrm((N_MOE, N_EXPERTS, D_FF_EXPERT, D_MODEL), D_FF_EXPERT ** -0.5 * out_scale),
        "norm_ple": gain((DEPTH, D_MODEL)),
        "w_ple_gate": nrm((DEPTH, D_MODEL, D_MODEL), D_MODEL ** -0.5),
        "w_ple_proj": nrm((DEPTH, PLE_DIM, D_MODEL), PLE_DIM ** -0.5 * out_scale),
        "final_norm": gain((D_MODEL,)),
    }


def reference(x_prompt, x_sample, state_lru_h, state_conv_a, state_conv_b, p_prompt, p_sample,
              norm_mix, w_in, conv_a_w, conv_a_b, lru_wx, lru_bx, lru_wa, lru_ba, lru_lambda,
              w_a_out, conv_b_w, w_b_out, w_o, norm_ffn, ffn_w_gu, ffn_w_dn, router,
              moe_w_gu, moe_w_dn, norm_ple, w_ple_gate, w_ple_proj, final_norm):

    def run(x, p, lru0, ca0, cb0):
        h = x
        lru_s, ca_s, cb_s = [], [], []
        for i in range(DEPTH):
            n = rms_norm(h, norm_mix[i])
            z = n @ w_in[i]
            xa, ya, xb, g_post, g_pre, m_a, m_b = jnp.split(z, SPLIT_IDX, axis=-1)
            ua, ca_new = causal_dwconv(xa, ca0[i], conv_a_w[i])
            ua = ua + conv_a_b[i]
            ha, h_last = rg_lru(ua, lru_wx[i], lru_bx[i], lru_wa[i], lru_ba[i], lru_lambda[i], lru0[i])
            out_a = (ha * jax.nn.gelu(ya)) @ w_a_out[i]
            ub, cb_new = causal_dwconv(g_pre * xb, cb0[i], conv_b_w[i])
            out_b = (g_post * ub) @ w_b_out[i]
            merged = jax.nn.sigmoid(m_a) * out_a + jax.nn.sigmoid(m_b) * out_b
            h = h + merged @ w_o[i]
            n2 = rms_norm(h, norm_ffn[i])
            if i % 2 == 0:
                h = h + swiglu(n2, ffn_w_gu[i // 2], ffn_w_dn[i // 2])
            else:
                h = h + moe_swiglu(n2, router[i // 2], moe_w_gu[i // 2], moe_w_dn[i // 2])
            e_i = p[i] @ w_ple_proj[i]
            h = h + jax.nn.sigmoid(rms_norm(h, norm_ple[i]) @ w_ple_gate[i]) * e_i
            lru_s.append(h_last.astype(lru0.dtype))
            ca_s.append(ca_new.astype(ca0.dtype))
            cb_s.append(cb_new.astype(cb0.dtype))
        return rms_norm(h, final_norm), jnp.stack(lru_s), jnp.stack(ca_s), jnp.stack(cb_s)

    dt = x_prompt.dtype
    lru0_p = jnp.zeros((DEPTH, x_prompt.shape[0], LRU_WIDTH), dt)
    ca0_p = jnp.zeros((DEPTH, x_prompt.shape[0], CONV_A_WIDTH - 1, LRU_WIDTH), dt)
    cb0_p = jnp.zeros((DEPTH, x_prompt.shape[0], CONV_B_WIDTH - 1, SC_WIDTH), dt)
    y_prompt, lru_p, ca_p, cb_p = run(x_prompt, p_prompt, lru0_p, ca0_p, cb0_p)
    y_sample, lru_d, ca_d, cb_d = run(x_sample, p_sample, state_lru_h, state_conv_a, state_conv_b)
    return (y_prompt, y_sample, lru_p, ca_p, cb_p, lru_d, ca_d, cb_d)
```

```python
import functools

import jax
import jax.numpy as jnp
from jax import lax
from jax.experimental import pallas as pl
from jax.experimental.pallas import tpu as pltpu

D_MODEL = 1024
DEPTH = 4
LRU_HEADS = 16
LRU_BLOCK = D_MODEL // LRU_HEADS
CONV_A_WIDTH = 4
CONV_B_WIDTH = 3
LRU_C = 8.0
PLE_DIM = 256
D_FF = 3 * D_MODEL
N_EXPERTS = 8
D_FF_EXPERT = 3 * D_MODEL // 2
EPS = 1e-6
IN_COLS = 7 * D_MODEL

LANES = 128
GATE_CHUNK = 256
N_GATE_CHUNKS = D_MODEL // GATE_CHUNK
HEADS_PER_CHUNK = GATE_CHUNK // LRU_BLOCK
FF_TILE = 512
VMEM_LIMIT = 56 * 1024 * 1024

F32 = jnp.float32
BF16 = jnp.bfloat16


def _rms(x, g):
    var = jnp.mean(x * x, axis=-1, keepdims=True)
    return x * lax.rsqrt(var + EPS) * g


def _sigmoid(x):
    return 1.0 / (1.0 + jnp.exp(-x))


def _gelu_tanh(x):
    c = 0.7978845608028654
    return 0.5 * x * (1.0 + jnp.tanh(c * (x + 0.044715 * (x * x * x))))


def _dot(a, b):
    return jnp.dot(a, b, preferred_element_type=F32)


def _mixer_kernel(G, TT,
                  h_ref, ca0_ref, cb0_ref, l0_ref, gmix_ref, win_ref, caw_ref, cab_ref,
                  wg_ref, bx_ref, ba_ref, lam_ref, wao_ref, cbw_ref, wbo_ref, wo_ref,
                  hout_ref, lru_ref, ca_ref, cb_ref,
                  xa_buf, ub_buf, a_buf, b_buf, hstate):
    TM = TT * G
    HA = (CONV_A_WIDTH - 1) * G
    HB = (CONV_B_WIDTH - 1) * G
    D = D_MODEL

    @pl.when(pl.program_id(0) == 0)
    def _():
        xa_buf[0:HA, :] = ca0_ref[...]
        ub_buf[0:HB, :] = cb0_ref[...]
        hstate[...] = l0_ref[...]

    h = h_ref[...]
    n = _rms(h, gmix_ref[...]).astype(BF16)

    def proj(c):
        return _dot(n, win_ref[:, c * D:(c + 1) * D])

    xa = proj(0)
    xa_buf[HA:HA + TM, :] = xa
    ua = caw_ref[0:1, :] * xa_buf[0:TM, :]
    ua = ua + caw_ref[1:2, :] * xa_buf[G:G + TM, :]
    ua = ua + caw_ref[2:3, :] * xa_buf[2 * G:2 * G + TM, :]
    ua = ua + caw_ref[3:4, :] * xa
    ua = ua + cab_ref[...]
    ua_bf = ua.astype(BF16)

    lam = lam_ref[...]
    log_sig = -(jnp.maximum(-lam, 0.0) + jnp.log1p(jnp.exp(-jnp.abs(lam))))
    for c in range(N_GATE_CHUNKS):
        sl = slice(c * GATE_CHUNK, (c + 1) * GATE_CHUNK)
        g = _dot(ua_bf[:, sl], wg_ref[c])
        gate_x = _sigmoid(g[:, :GATE_CHUNK] + bx_ref[:, sl])
        gate_a = _sigmoid(g[:, GATE_CHUNK:] + ba_ref[:, sl])
        log_a = LRU_C * gate_a * log_sig[:, sl]
        a = jnp.exp(log_a)
        a_buf[:, sl] = a
        b_buf[:, sl] = jnp.sqrt(1.0 - a * a) * gate_x * ua[:, sl]

    def step(t, hc):
        r = pl.multiple_of(t * G, G)
        hn = a_buf[pl.ds(r, G), :] * hc + b_buf[pl.ds(r, G), :]
        b_buf[pl.ds(r, G), :] = hn
        return hn

    h_last = lax.fori_loop(0, TT, step, hstate[...], unroll=min(TT, 8))
    hstate[...] = h_last

    gated_a = (b_buf[...] * _gelu_tanh(proj(1))).astype(BF16)
    out_a = _dot(gated_a, wao_ref[...])

    ub_buf[HB:HB + TM, :] = proj(4) * proj(2)
    ub = cbw_ref[0:1, :] * ub_buf[0:TM, :]
    ub = ub + cbw_ref[1:2, :] * ub_buf[G:G + TM, :]
    ub = ub + cbw_ref[2:3, :] * ub_buf[2 * G:2 * G + TM, :]
    gated_b = (proj(3) * ub).astype(BF16)
    out_b = _dot(gated_b, wbo_ref[...])

    merged = _sigmoid(proj(5)) * out_a + _sigmoid(proj(6)) * out_b
    hout_ref[...] = h + _dot(merged.astype(BF16), wo_ref[...])

    new_ca = xa_buf[TM:TM + HA, :]
    new_cb = ub_buf[TM:TM + HB, :]
    xa_buf[0:HA, :] = new_ca
    ub_buf[0:HB, :] = new_cb
    ca_ref[...] = new_ca
    cb_ref[...] = new_cb
    lru_ref[...] = h_last


def _const_spec(shape, layer=None):
    if layer is None:
        return pl.BlockSpec(shape, lambda *_: (0,) * len(shape), pipeline_mode=pl.Buffered(1))
    return pl.BlockSpec((None,) + shape, lambda *_: (layer,) + (0,) * len(shape),
                        pipeline_mode=pl.Buffered(1))


def _mixer(layer, G, TT, h, ca0, cb0, l0, W):
    N = h.shape[0]
    TM = TT * G
    D = D_MODEL
    HA = (CONV_A_WIDTH - 1) * G
    HB = (CONV_B_WIDTH - 1) * G
    row = pl.BlockSpec((TM, D), lambda i: (i, 0))
    vec = _const_spec((1, D), layer)
    in_specs = [
        row,
        _const_spec((HA, D)), _const_spec((HB, D)), _const_spec((G, D)),
        vec,
        _const_spec((D, IN_COLS), layer),
        _const_spec((CONV_A_WIDTH, D), layer), vec,
        _const_spec((N_GATE_CHUNKS, GATE_CHUNK, 2 * GATE_CHUNK), layer),
        vec, vec, vec,
        _const_spec((D, D), layer),
        _const_spec((CONV_B_WIDTH, D), layer),
        _const_spec((D, D), layer),
        _const_spec((D, D), layer),
    ]
    out_specs = [row, _const_spec((G, D)), _const_spec((HA, D)), _const_spec((HB, D))]
    out_shape = [jax.ShapeDtypeStruct((N, D), F32), jax.ShapeDtypeStruct((G, D), F32),
                 jax.ShapeDtypeStruct((HA, D), F32), jax.ShapeDtypeStruct((HB, D), F32)]
    scratch = [pltpu.VMEM((HA + TM, D), F32), pltpu.VMEM((HB + TM, D), F32),
               pltpu.VMEM((TM, D), F32), pltpu.VMEM((TM, D), F32), pltpu.VMEM((G, D), F32)]
    return pl.pallas_call(
        functools.partial(_mixer_kernel, G, TT),
        grid=(N // TM,),
        in_specs=in_specs, out_specs=out_specs, out_shape=out_shape,
        scratch_shapes=scratch,
        compiler_params=pltpu.CompilerParams(dimension_semantics=("arbitrary",),
                                             vmem_limit_bytes=VMEM_LIMIT),
        name=f"mixer_g{G}",
    )(h, ca0, cb0, l0, W["norm_mix"], W["w_in"], W["conv_a_w"], W["conv_a_b"], W["w_gate"],
      W["lru_bx"], W["lru_ba"], W["lru_lambda"], W["w_a_out"], W["conv_b_w"], W["w_b_out"], W["w_o"])


def _ffn_kernel(moe, nk, *refs):
    if moe:
        (h_ref, gffn_ref, router_ref, wg_ref, wu_ref, wd_ref, gple_ref, wpg_ref, p_ref, wpp_ref,
         hout_ref, n2_buf, acc, comb_buf) = refs
    else:
        (h_ref, gffn_ref, wg_ref, wu_ref, wd_ref, gple_ref, wpg_ref, p_ref, wpp_ref,
         hout_ref, n2_buf, acc) = refs
    k = pl.program_id(1)
    TM = h_ref.shape[0]

    @pl.when(k == 0)
    def _():
        n2 = _rms(h_ref[...], gffn_ref[...]).astype(BF16)
        n2_buf[...] = n2
        acc[...] = jnp.zeros_like(acc)
        if moe:
            lane = lax.broadcasted_iota(jnp.int32, (TM, LANES), 1)
            logits = jnp.where(lane < N_EXPERTS, _dot(n2, router_ref[...]), -jnp.inf)
            v1 = jnp.max(logits, axis=-1, keepdims=True)
            i1 = jnp.min(jnp.where(logits == v1, lane, LANES), axis=-1, keepdims=True)
            rest = jnp.where(lane == i1, -jnp.inf, logits)
            v2 = jnp.max(rest, axis=-1, keepdims=True)
            i2 = jnp.min(jnp.where(rest == v2, lane, LANES), axis=-1, keepdims=True)
            ex = jnp.exp(v2 - v1)
            w1 = 1.0 / (1.0 + ex)
            w2 = ex / (1.0 + ex)
            for e in range(N_EXPERTS):
                ce = jnp.where(i1 == e, w1, 0.0) + jnp.where(i2 == e, w2, 0.0)
                comb_buf[e] = jnp.broadcast_to(ce, (TM, LANES))

    x = n2_buf[...]
    g = _dot(x, wg_ref[...])
    u = _dot(x, wu_ref[...])
    act = g * _sigmoid(g) * u
    if moe:
        ce = comb_buf[k // (D_FF_EXPERT // FF_TILE)]
        act = act * jnp.concatenate([ce] * (FF_TILE // LANES), axis=1)
    acc[...] += _dot(act.astype(BF16), wd_ref[...])

    @pl.when(k == nk - 1)
    def _():
        h1 = h_ref[...] + acc[...]
        n3 = _rms(h1, gple_ref[...]).astype(BF16)
        gate = _sigmoid(_dot(n3, wpg_ref[...]))
        hout_ref[...] = h1 + gate * _dot(p_ref[...], wpp_ref[...])


def _ffn(layer, TM, h, p, W):
    N = h.shape[0]
    D = D_MODEL
    moe = layer % 2 == 1
    j = layer // 2
    row = pl.BlockSpec((TM, D), lambda i, k: (i, 0))
    vec = _const_spec((1, D), layer)
    if moe:
        per = D_FF_EXPERT // FF_TILE
        nk = N_EXPERTS * per
        w_specs = [
            pl.BlockSpec((None, None, D, FF_TILE), lambda i, k: (j, k // per, 0, k % per)),
            pl.BlockSpec((None, None, D, FF_TILE), lambda i, k: (j, k // per, 0, per + k % per)),
            pl.BlockSpec((None, None, FF_TILE, D), lambda i, k: (j, k // per, k % per, 0)),
        ]
        weights = [W["moe_w_gu"], W["moe_w_gu"], W["moe_w_dn"]]
        route_specs, route_args = [_const_spec((D, LANES), j)], [W["router"]]
        extra_scratch = [pltpu.VMEM((N_EXPERTS, TM, LANES), F32)]
    else:
        nk = D_FF // FF_TILE
        w_specs = [
            pl.BlockSpec((None, D, FF_TILE), lambda i, k: (j, 0, k)),
            pl.BlockSpec((None, D, FF_TILE), lambda i, k: (j, 0, nk + k)),
            pl.BlockSpec((None, FF_TILE, D), lambda i, k: (j, k, 0)),
        ]
        weights = [W["ffn_w_gu"], W["ffn_w_gu"], W["ffn_w_dn"]]
        route_specs, route_args, extra_scratch = [], [], []
    in_specs = ([row, vec] + route_specs + w_specs
                + [vec, _const_spec((D, D), layer),
                   pl.BlockSpec((None, TM, PLE_DIM), lambda i, k: (layer, i, 0)),
                   _const_spec((PLE_DIM, D), layer)])
    args = ([h, W["norm_ffn"]] + route_args + weights
            + [W["norm_ple"], W["w_ple_gate"], p, W["w_ple_proj"]])
    return pl.pallas_call(
        functools.partial(_ffn_kernel, moe, nk),
        grid=(N // TM, nk),
        in_specs=in_specs, out_specs=row,
        out_shape=jax.ShapeDtypeStruct((N, D), F32),
        scratch_shapes=[pltpu.VMEM((TM, D), BF16), pltpu.VMEM((TM, D), F32)] + extra_scratch,
        compiler_params=pltpu.CompilerParams(dimension_semantics=("parallel", "arbitrary"),
                                             vmem_limit_bytes=VMEM_LIMIT),
        name=f"ffn_{'moe' if moe else 'dense'}_{TM}",
    )(*args)


def _norm_kernel(h_ref, g_ref, o_ref):
    o_ref[...] = _rms(h_ref[...], g_ref[...])


def _final_norm(TM, h, g):
    N = h.shape[0]
    row = pl.BlockSpec((TM, D_MODEL), lambda i: (i, 0))
    return pl.pallas_call(
        _norm_kernel, grid=(N // TM,),
        in_specs=[row, _const_spec((1, D_MODEL))], out_specs=row,
        out_shape=jax.ShapeDtypeStruct((N, D_MODEL), F32),
        compiler_params=pltpu.CompilerParams(dimension_semantics=("parallel",)),
        name=f"final_norm_{TM}",
    )(h, g)


def _block_diag_gates(wx, wa):
    eye = jnp.eye(HEADS_PER_CHUNK, dtype=wx.dtype)

    def bd(w):
        w = w.reshape(DEPTH, N_GATE_CHUNKS, HEADS_PER_CHUNK, LRU_BLOCK, LRU_BLOCK)
        w = jnp.einsum("dchij,hk->dchikj", w, eye)
        return w.reshape(DEPTH, N_GATE_CHUNKS, GATE_CHUNK, GATE_CHUNK)

    return jnp.concatenate([bd(wx), bd(wa)], axis=-1).astype(BF16)


def _run(h, p, G, TT, ffn_tm, lru0, ca0, cb0, W):
    lru_s, ca_s, cb_s = [], [], []
    for layer in range(DEPTH):
        h, lru, ca, cb = _mixer(layer, G, TT, h, ca0[layer], cb0[layer], lru0[layer], W)
        h = _ffn(layer, ffn_tm, h, p, W)
        lru_s.append(lru)
        ca_s.append(ca)
        cb_s.append(cb)
    y = _final_norm(ffn_tm, h, W["final_norm"])
    return y, jnp.stack(lru_s), jnp.stack(ca_s), jnp.stack(cb_s)


def kernel(x_prompt, x_sample, state_lru_h, state_conv_a, state_conv_b, p_prompt, p_sample,
           norm_mix, w_in, conv_a_w, conv_a_b, lru_wx, lru_bx, lru_wa, lru_ba, lru_lambda,
           w_a_out, conv_b_w, w_b_out, w_o, norm_ffn, ffn_w_gu, ffn_w_dn, router,
           moe_w_gu, moe_w_dn, norm_ple, w_ple_gate, w_ple_proj, final_norm):
    B, T, D = x_prompt.shape
    S = x_sample.shape[0]
    KA, KB = CONV_A_WIDTH - 1, CONV_B_WIDTH - 1

    def vec(v):
        return v.reshape(v.shape[0], 1, D)

    W = {
        "norm_mix": vec(norm_mix), "w_in": w_in.astype(BF16),
        "conv_a_w": conv_a_w, "conv_a_b": vec(conv_a_b),
        "w_gate": _block_diag_gates(lru_wx, lru_wa),
        "lru_bx": vec(lru_bx), "lru_ba": vec(lru_ba), "lru_lambda": vec(lru_lambda),
        "w_a_out": w_a_out.astype(BF16), "conv_b_w": conv_b_w, "w_b_out": w_b_out.astype(BF16),
        "w_o": w_o.astype(BF16), "norm_ffn": vec(norm_ffn),
        "ffn_w_gu": ffn_w_gu.astype(BF16), "ffn_w_dn": ffn_w_dn.astype(BF16),
        "router": jnp.pad(router, ((0, 0), (0, 0), (0, LANES - N_EXPERTS))).astype(BF16),
        "moe_w_gu": moe_w_gu.astype(BF16), "moe_w_dn": moe_w_dn.astype(BF16),
        "norm_ple": vec(norm_ple), "w_ple_gate": w_ple_gate.astype(BF16),
        "w_ple_proj": w_ple_proj.astype(BF16), "final_norm": final_norm.reshape(1, D),
    }

    hp = x_prompt.transpose(1, 0, 2).reshape(T * B, D)
    pp = p_prompt.transpose(0, 2, 1, 3).reshape(DEPTH, T * B, PLE_DIM).astype(BF16)
    zeros = lambda k: jnp.zeros((DEPTH, k * B, D), F32)
    y_p, lru_p, ca_p, cb_p = _run(hp, pp, B, 32, 1024, zeros(1), zeros(KA), zeros(KB), W)
    y_p = y_p.reshape(T, B, D).transpose(1, 0, 2)
    ca_p = ca_p.reshape(DEPTH, KA, B, D).transpose(0, 2, 1, 3)
    cb_p = cb_p.reshape(DEPTH, KB, B, D).transpose(0, 2, 1, 3)

    hs = x_sample.reshape(S, D)
    ps = p_sample.reshape(DEPTH, S, PLE_DIM).astype(BF16)
    ca0 = state_conv_a.transpose(0, 2, 1, 3).reshape(DEPTH, KA * S, D)
    cb0 = state_conv_b.transpose(0, 2, 1, 3).reshape(DEPTH, KB * S, D)
    y_s, lru_d, ca_d, cb_d = _run(hs, ps, S, 1, S, state_lru_h, ca0, cb0, W)
    y_s = y_s.reshape(S, 1, D)
    ca_d = ca_d.reshape(DEPTH, KA, S, D).transpose(0, 2, 1, 3)
    cb_d = cb_d.reshape(DEPTH, KB, S, D).transpose(0, 2, 1, 3)

    return (y_p, y_s, lru_p, ca_p, cb_p, lru_d, ca_d, cb_d)
```

```python
import functools

import jax
import jax.numpy as jnp
from jax import lax
from jax.experimental import pallas as pl
from jax.experimental.pallas import tpu as pltpu

D_MODEL = 1024
DEPTH = 4
LRU_HEADS = 16
LRU_BLOCK = D_MODEL // LRU_HEADS
CONV_A_WIDTH = 4
CONV_B_WIDTH = 3
LRU_C = 8.0
PLE_DIM = 256
D_FF = 3 * D_MODEL
N_EXPERTS = 8
D_FF_EXPERT = 3 * D_MODEL // 2
EPS = 1e-6
IN_COLS = 7 * D_MODEL

LANES = 128
GATE_CHUNK = 256
N_GATE_CHUNKS = D_MODEL // GATE_CHUNK
HEADS_PER_CHUNK = GATE_CHUNK // LRU_BLOCK
FF_TILE = 512
VMEM_LIMIT = 56 * 1024 * 1024

F32 = jnp.float32
BF16 = jnp.bfloat16


def _rms(x, g):
    var = jnp.mean(x * x, axis=-1, keepdims=True)
    return x * lax.rsqrt(var + EPS) * g


def _sigmoid(x):
    return 1.0 / (1.0 + jnp.exp(-x))


def _gelu_tanh(x):
    c = 0.7978845608028654
    return 0.5 * x * (1.0 + jnp.tanh(c * (x + 0.044715 * (x * x * x))))


def _dot(a, b):
    return jnp.dot(a, b, preferred_element_type=F32)


def _mixer_kernel(G, TT,
                  h_ref, ca0_ref, cb0_ref, l0_ref, gmix_ref, win_ref, caw_ref, cab_ref,
                  wg_ref, bx_ref, ba_ref, lam_ref, wao_ref, cbw_ref, wbo_ref, wo_ref,
                  hout_ref, lru_ref, ca_ref, cb_ref,
                  xa_buf, ub_buf, a_buf, b_buf, hstate):
    TM = TT * G
    HA = (CONV_A_WIDTH - 1) * G
    HB = (CONV_B_WIDTH - 1) * G
    D = D_MODEL

    @pl.when(pl.program_id(0) == 0)
    def _():
        xa_buf[0:HA, :] = ca0_ref[...]
        ub_buf[0:HB, :] = cb0_ref[...]
        hstate[...] = l0_ref[...]

    h = h_ref[...]
    n = _rms(h, gmix_ref[...]).astype(BF16)

    def proj(c):
        return _dot(n, win_ref[:, c * D:(c + 1) * D])

    xa = proj(0)
    xa_buf[HA:HA + TM, :] = xa
    ua = caw_ref[0:1, :] * xa_buf[0:TM, :]
    ua = ua + caw_ref[1:2, :] * xa_buf[G:G + TM, :]
    ua = ua + caw_ref[2:3, :] * xa_buf[2 * G:2 * G + TM, :]
    ua = ua + caw_ref[3:4, :] * xa
    ua = ua + cab_ref[...]
    ua_bf = ua.astype(BF16)

    lam = lam_ref[...]
    log_sig = -(jnp.maximum(-lam, 0.0) + jnp.log1p(jnp.exp(-jnp.abs(lam))))
    for c in range(N_GATE_CHUNKS):
        sl = slice(c * GATE_CHUNK, (c + 1) * GATE_CHUNK)
        g = _dot(ua_bf[:, sl], wg_ref[c])
        gate_x = _sigmoid(g[:, :GATE_CHUNK] + bx_ref[:, sl])
        gate_a = _sigmoid(g[:, GATE_CHUNK:] + ba_ref[:, sl])
        log_a = LRU_C * gate_a * log_sig[:, sl]
        a = jnp.exp(log_a)
        a_buf[:, sl] = a
        b_buf[:, sl] = jnp.sqrt(1.0 - a * a) * gate_x * ua[:, sl]

    def step(t, hc):
        r = pl.multiple_of(t * G, G)
        hn = a_buf[pl.ds(r, G), :] * hc + b_buf[pl.ds(r, G), :]
        b_buf[pl.ds(r, G), :] = hn
        return hn

    h_last = lax.fori_loop(0, TT, step, hstate[...], unroll=min(TT, 8))
    hstate[...] = h_last

    gated_a = (b_buf[...] * _gelu_tanh(proj(1))).astype(BF16)
    out_a = _dot(gated_a, wao_ref[...])

    ub_buf[HB:HB + TM, :] = proj(4) * proj(2)
    ub = cbw_ref[0:1, :] * ub_buf[0:TM, :]
    ub = ub + cbw_ref[1:2, :] * ub_buf[G:G + TM, :]
    ub = ub + cbw_ref[2:3, :] * ub_buf[2 * G:2 * G + TM, :]
    gated_b = (proj(3) * ub).astype(BF16)
    out_b = _dot(gated_b, wbo_ref[...])

    merged = _sigmoid(proj(5)) * out_a + _sigmoid(proj(6)) * out_b
    hout_ref[...] = h + _dot(merged.astype(BF16), wo_ref[...])

    new_ca = xa_buf[TM:TM + HA, :]
    new_cb = ub_buf[TM:TM + HB, :]
    xa_buf[0:HA, :] = new_ca
    ub_buf[0:HB, :] = new_cb
    ca_ref[...] = new_ca
    cb_ref[...] = new_cb
    lru_ref[...] = h_last


def _const_spec(shape, layer=None):
    if layer is None:
        return pl.BlockSpec(shape, lambda *_: (0,) * len(shape), pipeline_mode=pl.Buffered(1))
    return pl.BlockSpec((None,) + shape, lambda *_: (layer,) + (0,) * len(shape),
                        pipeline_mode=pl.Buffered(1))


def _mixer(layer, G, TT, h, ca0, cb0, l0, W):
    N = h.shape[0]
    TM = TT * G
    D = D_MODEL
    HA = (CONV_A_WIDTH - 1) * G
    HB = (CONV_B_WIDTH - 1) * G
    row = pl.BlockSpec((TM, D), lambda i: (i, 0))
    vec = _const_spec((1, D), layer)
    in_specs = [
        row,
        _const_spec((HA, D)), _const_spec((HB, D)), _const_spec((G, D)),
        vec,
        _const_spec((D, IN_COLS), layer),
        _const_spec((CONV_A_WIDTH, D), layer), vec,
        _const_spec((N_GATE_CHUNKS, GATE_CHUNK, 2 * GATE_CHUNK), layer),
        vec, vec, vec,
        _const_spec((D, D), layer),
        _const_spec((CONV_B_WIDTH, D), layer),
        _const_spec((D, D), layer),
        _const_spec((D, D), layer),
    ]
    out_specs = [row, _const_spec((G, D)), _const_spec((HA, D)), _const_spec((HB, D))]
    out_shape = [jax.ShapeDtypeStruct((N, D), F32), jax.ShapeDtypeStruct((G, D), F32),
                 jax.ShapeDtypeStruct((HA, D), F32), jax.ShapeDtypeStruct((HB, D), F32)]
    scratch = [pltpu.VMEM((HA + TM, D), F32), pltpu.VMEM((HB + TM, D), F32),
               pltpu.VMEM((TM, D), F32), pltpu.VMEM((TM, D), F32), pltpu.VMEM((G, D), F32)]
    return pl.pallas_call(
        functools.partial(_mixer_kernel, G, TT),
        grid=(N // TM,),
        in_specs=in_specs, out_specs=out_specs, out_shape=out_shape,
        scratch_shapes=scratch,
        compiler_params=pltpu.CompilerParams(dimension_semantics=("arbitrary",),
                                             vmem_limit_bytes=VMEM_LIMIT),
        name=f"mixer_g{G}",
    )(h, ca0, cb0, l0, W["norm_mix"], W["w_in"], W["conv_a_w"], W["conv_a_b"], W["w_gate"],
      W["lru_bx"], W["lru_ba"], W["lru_lambda"], W["w_a_out"], W["conv_b_w"], W["w_b_out"], W["w_o"])


def _ple(h1, gple_ref, wpg_ref, p_ref, wpp_ref):
    n3 = _rms(h1, gple_ref[...]).astype(BF16)
    gate = _sigmoid(_dot(n3, wpg_ref[...]))
    return h1 + gate * _dot(p_ref[...], wpp_ref[...])


def _ple_specs(layer, TM, ngrid):
    if ngrid == 1:
        p_spec = pl.BlockSpec((None, TM, PLE_DIM), lambda i: (layer, i, 0))
    else:
        p_spec = pl.BlockSpec((None, TM, PLE_DIM), lambda i, k: (layer, i, 0))
    return [_const_spec((1, D_MODEL), layer), _const_spec((D_MODEL, D_MODEL), layer), p_spec,
            _const_spec((PLE_DIM, D_MODEL), layer)]


def _ple_args(p, W):
    return [W["norm_ple"], W["w_ple_gate"], p, W["w_ple_proj"]]


def _ffn_kernel(nk, h_ref, gffn_ref, wg_ref, wu_ref, wd_ref, gple_ref, wpg_ref, p_ref, wpp_ref,
                hout_ref, n2_buf, acc):
    k = pl.program_id(1)

    @pl.when(k == 0)
    def _():
        n2_buf[...] = _rms(h_ref[...], gffn_ref[...]).astype(BF16)
        acc[...] = jnp.zeros_like(acc)

    x = n2_buf[...]
    g = _dot(x, wg_ref[...])
    u = _dot(x, wu_ref[...])
    acc[...] += _dot((g * _sigmoid(g) * u).astype(BF16), wd_ref[...])

    @pl.when(k == nk - 1)
    def _():
        hout_ref[...] = _ple(h_ref[...] + acc[...], gple_ref, wpg_ref, p_ref, wpp_ref)


def _ffn_dense(layer, TM, h, p, W):
    N = h.shape[0]
    D = D_MODEL
    j = layer // 2
    nk = D_FF // FF_TILE
    row = pl.BlockSpec((TM, D), lambda i, k: (i, 0))
    in_specs = [row, _const_spec((1, D), layer),
                pl.BlockSpec((None, D, FF_TILE), lambda i, k: (j, 0, k)),
                pl.BlockSpec((None, D, FF_TILE), lambda i, k: (j, 0, nk + k)),
                pl.BlockSpec((None, FF_TILE, D), lambda i, k: (j, k, 0))] + _ple_specs(layer, TM, 2)
    return pl.pallas_call(
        functools.partial(_ffn_kernel, nk),
        grid=(N // TM, nk),
        in_specs=in_specs, out_specs=row,
        out_shape=jax.ShapeDtypeStruct((N, D), F32),
        scratch_shapes=[pltpu.VMEM((TM, D), BF16), pltpu.VMEM((TM, D), F32)],
        compiler_params=pltpu.CompilerParams(dimension_semantics=("parallel", "arbitrary"),
                                             vmem_limit_bytes=VMEM_LIMIT),
        name=f"ffn_dense_{TM}",
    )(h, W["norm_ffn"], W["ffn_w_gu"], W["ffn_w_gu"], W["ffn_w_dn"], *_ple_args(p, W))


def _route_kernel(h_ref, gffn_ref, router_ref, n2_ref, idx_ref, wts_ref, cnt_ref, carry):
    TM = h_ref.shape[0]

    @pl.when(pl.program_id(0) == 0)
    def _():
        carry[...] = jnp.zeros_like(carry)

    n2 = _rms(h_ref[...], gffn_ref[...])
    n2_ref[...] = n2
    lane = lax.broadcasted_iota(jnp.int32, (TM, LANES), 1)
    logits = jnp.where(lane < N_EXPERTS, _dot(n2.astype(BF16), router_ref[...]), -jnp.inf)
    v1 = jnp.max(logits, axis=-1, keepdims=True)
    i1 = jnp.min(jnp.where(logits == v1, lane, LANES), axis=-1, keepdims=True)
    rest = jnp.where(lane == i1, -jnp.inf, logits)
    v2 = jnp.max(rest, axis=-1, keepdims=True)
    i2 = jnp.min(jnp.where(rest == v2, lane, LANES), axis=-1, keepdims=True)
    ex = jnp.exp(v2 - v1)
    w1 = 1.0 / (1.0 + ex)
    w2 = ex / (1.0 + ex)
    m1 = lane == i1
    m2 = lane == i2
    chosen = jnp.where(m1 | m2, 1.0, 0.0)
    before = (lax.broadcasted_iota(jnp.int32, (TM, TM), 1)
              < lax.broadcasted_iota(jnp.int32, (TM, TM), 0))
    counts = _dot(jnp.where(before, 1.0, 0.0).astype(BF16), chosen.astype(BF16)) + carry[...]
    r1 = jnp.sum(jnp.where(m1, counts, 0.0), axis=-1, keepdims=True).astype(jnp.int32)
    r2 = jnp.sum(jnp.where(m2, counts, 0.0), axis=-1, keepdims=True).astype(jnp.int32)
    carry[...] += jnp.sum(chosen, axis=0, keepdims=True)
    cnt_ref[...] = carry[...].astype(jnp.int32)
    idx_ref[...] = jnp.where(lane == 0, i1, jnp.where(lane == 1, i2,
                             jnp.where(lane == 2, r1, jnp.where(lane == 3, r2, 0))))
    wts_ref[...] = jnp.where(lane == 0, w1, jnp.where(lane == 1, w2, 0.0))


def _route(layer, TM, h, W):
    N = h.shape[0]
    D = D_MODEL
    row = pl.BlockSpec((TM, D), lambda i: (i, 0))
    info = pl.BlockSpec((TM, LANES), lambda i: (i, 0))
    return pl.pallas_call(
        _route_kernel, grid=(N // TM,),
        in_specs=[row, _const_spec((1, D), layer), _const_spec((D, LANES), layer // 2)],
        out_specs=[row, info, info, _const_spec((1, LANES))],
        out_shape=[jax.ShapeDtypeStruct((N, D), F32), jax.ShapeDtypeStruct((N, LANES), jnp.int32),
                   jax.ShapeDtypeStruct((N, LANES), F32), jax.ShapeDtypeStruct((1, LANES), jnp.int32)],
        scratch_shapes=[pltpu.VMEM((1, LANES), F32)],
        compiler_params=pltpu.CompilerParams(dimension_semantics=("arbitrary",)),
        name=f"moe_route_{TM}",
    )(h, W["norm_ffn"], W["router"])


def _row_copy(src, s, dst, d, sem):
    return pltpu.make_async_copy(src.at[pl.ds(s, 1)], dst.at[pl.ds(d, 1)], sem)


def _dispatch_kernel(TM, TE, n_tiles, cnt_ref, off_ref, pos_ref, n2_hbm, xs_hbm, zero_buf, sem, pad_sem):
    i = pl.program_id(0)

    @pl.when(i == 0)
    def _():
        zero_buf[...] = jnp.zeros_like(zero_buf)
        for e in range(N_EXPERTS):
            lo = off_ref[e] + cnt_ref[e]
            hi = off_ref[e + 1]

            def pad(r, c):
                _row_copy(zero_buf, 0, xs_hbm, r, pad_sem).start()
                return c

            def pad_wait(r, c):
                _row_copy(zero_buf, 0, xs_hbm, 0, pad_sem).wait()
                return c

            lax.fori_loop(lo, hi, pad, 0)
            lax.fori_loop(lo, hi, pad_wait, 0)

        def tile_copy(t):
            return pltpu.make_async_copy(zero_buf, xs_hbm.at[pl.ds(pl.multiple_of(t * TE, TE), TE)], pad_sem)

        def fill(t, c):
            tile_copy(t).start()
            return c

        def fill_wait(t, c):
            tile_copy(t).wait()
            return c

        first_unused = off_ref[N_EXPERTS] // TE
        lax.fori_loop(first_unused, n_tiles, fill, 0)
        lax.fori_loop(first_unused, n_tiles, fill_wait, 0)

    base = i * TM

    def issue(r, c):
        _row_copy(n2_hbm, base + r, xs_hbm, pos_ref[0, 0, 2 * r], sem).start()
        _row_copy(n2_hbm, base + r, xs_hbm, pos_ref[0, 0, 2 * r + 1], sem).start()
        return c

    def drain(r, c):
        _row_copy(n2_hbm, 0, xs_hbm, 0, sem).wait()
        _row_copy(n2_hbm, 0, xs_hbm, 0, sem).wait()
        return c

    lax.fori_loop(0, TM, issue, 0, unroll=8)
    lax.fori_loop(0, TM, drain, 0, unroll=8)


def _dispatch(TM, TE, n_tiles, cnt, off, pos, n2):
    N, D = n2.shape
    return pl.pallas_call(
        functools.partial(_dispatch_kernel, TM, TE, n_tiles),
        grid_spec=pltpu.PrefetchScalarGridSpec(
            num_scalar_prefetch=2, grid=(N // TM,),
            in_specs=[pl.BlockSpec((1, 1, 2 * TM), lambda i, c, o: (i, 0, 0), memory_space=pltpu.SMEM),
                      pl.BlockSpec(memory_space=pl.ANY)],
            out_specs=pl.BlockSpec(memory_space=pl.ANY),
            scratch_shapes=[pltpu.VMEM((TE, D), F32), pltpu.SemaphoreType.DMA, pltpu.SemaphoreType.DMA]),
        out_shape=jax.ShapeDtypeStruct((n_tiles * TE, D), F32),
        compiler_params=pltpu.CompilerParams(dimension_semantics=("arbitrary",),
                                             disable_bounds_checks=True),
        name=f"moe_dispatch_{TM}",
    )(cnt, off, pos, n2)


def _expert_kernel(te_ref, na_ref, x_ref, wgu_ref, wdn_ref, y_ref):
    used = pl.program_id(0) < na_ref[0]

    @pl.when(used)
    def _():
        gu = _dot(x_ref[...].astype(BF16), wgu_ref[...])
        g = gu[:, :D_FF_EXPERT]
        u = gu[:, D_FF_EXPERT:]
        y_ref[...] = _dot((g * _sigmoid(g) * u).astype(BF16), wdn_ref[...])

    @pl.when(jnp.logical_not(used))
    def _():
        y_ref[...] = jnp.zeros_like(y_ref)


def _experts(layer, TE, te, na, xs, W):
    P, D = xs.shape
    j = layer // 2
    rows = pl.BlockSpec((TE, D), lambda t, te, na: (t, 0))
    return pl.pallas_call(
        _expert_kernel,
        grid_spec=pltpu.PrefetchScalarGridSpec(
            num_scalar_prefetch=2, grid=(P // TE,),
            in_specs=[rows,
                      pl.BlockSpec((None, None, D, 2 * D_FF_EXPERT), lambda t, te, na: (j, te[t], 0, 0)),
                      pl.BlockSpec((None, None, D_FF_EXPERT, D), lambda t, te, na: (j, te[t], 0, 0))],
            out_specs=rows),
        out_shape=jax.ShapeDtypeStruct((P, D), F32),
        compiler_params=pltpu.CompilerParams(dimension_semantics=("arbitrary",),
                                             vmem_limit_bytes=VMEM_LIMIT),
        name=f"moe_experts_{TE}",
    )(te, na, xs, W["moe_w_gu"], W["moe_w_dn"])


def _combine_kernel(TM, pos_ref, h_ref, wts_ref, ys_hbm, gple_ref, wpg_ref, p_ref, wpp_ref,
                    hout_ref, y_buf, sem):
    def issue(r, c):
        _row_copy(ys_hbm, pos_ref[0, 0, 2 * r], y_buf.at[0], r, sem).start()
        _row_copy(ys_hbm, pos_ref[0, 0, 2 * r + 1], y_buf.at[1], r, sem).start()
        return c

    def drain(r, c):
        _row_copy(ys_hbm, 0, y_buf.at[0], 0, sem).wait()
        _row_copy(ys_hbm, 0, y_buf.at[1], 0, sem).wait()
        return c

    lax.fori_loop(0, TM, issue, 0, unroll=8)
    lax.fori_loop(0, TM, drain, 0, unroll=8)
    moe = wts_ref[:, 0:1] * y_buf[0] + wts_ref[:, 1:2] * y_buf[1]
    hout_ref[...] = _ple(h_ref[...] + moe, gple_ref, wpg_ref, p_ref, wpp_ref)


def _combine(layer, TM, pos, h, wts, ys, p, W):
    N, D = h.shape
    row = pl.BlockSpec((TM, D), lambda i: (i, 0))
    in_specs = [pl.BlockSpec((1, 1, 2 * TM), lambda i: (i, 0, 0), memory_space=pltpu.SMEM),
                row, pl.BlockSpec((TM, LANES), lambda i: (i, 0)),
                pl.BlockSpec(memory_space=pl.ANY)] + _ple_specs(layer, TM, 1)
    return pl.pallas_call(
        functools.partial(_combine_kernel, TM),
        grid=(N // TM,),
        in_specs=in_specs, out_specs=row,
        out_shape=jax.ShapeDtypeStruct((N, D), F32),
        scratch_shapes=[pltpu.VMEM((2, TM, D), F32), pltpu.SemaphoreType.DMA],
        compiler_params=pltpu.CompilerParams(dimension_semantics=("arbitrary",),
                                             vmem_limit_bytes=VMEM_LIMIT, disable_bounds_checks=True),
        name=f"moe_combine_{TM}",
    )(pos, h, wts, ys, *_ple_args(p, W))


def _ffn_moe(layer, TM, TE, h, p, W):
    N = h.shape[0]
    n_tiles = 2 * N // TE + N_EXPERTS
    n2, idx, wts, cnt = _route(layer, TM, h, W)
    cnt = cnt[0, :N_EXPERTS]
    ends = jnp.cumsum((cnt + TE - 1) // TE).astype(jnp.int32)
    off = jnp.concatenate([jnp.zeros((1,), jnp.int32), ends * TE])
    experts = jnp.arange(N_EXPERTS, dtype=jnp.int32)

    def first_row(e):
        return jnp.sum(jnp.where(e[:, None] == experts[None, :], off[None, :N_EXPERTS], 0), axis=1)

    pos = jnp.stack([first_row(idx[:, 0]) + idx[:, 2], first_row(idx[:, 1]) + idx[:, 3]], axis=1)
    pos = pos.reshape(N // TM, 1, 2 * TM)
    t = jnp.minimum(jnp.arange(n_tiles, dtype=jnp.int32), ends[-1] - 1)
    te = jnp.sum(t[:, None] >= ends[None, :], axis=1).astype(jnp.int32)
    xs = _dispatch(TM, TE, n_tiles, cnt, off, pos, n2)
    ys = _experts(layer, TE, te, ends[-1:], xs, W)
    return _combine(layer, TM, pos, h, wts, ys, p, W)


def _norm_kernel(h_ref, g_ref, o_ref):
    o_ref[...] = _rms(h_ref[...], g_ref[...])


def _final_norm(TM, h, g):
    N = h.shape[0]
    row = pl.BlockSpec((TM, D_MODEL), lambda i: (i, 0))
    return pl.pallas_call(
        _norm_kernel, grid=(N // TM,),
        in_specs=[row, _const_spec((1, D_MODEL))], out_specs=row,
        out_shape=jax.ShapeDtypeStruct((N, D_MODEL), F32),
        compiler_params=pltpu.CompilerParams(dimension_semantics=("parallel",)),
        name=f"final_norm_{TM}",
    )(h, g)


def _block_diag_gates(wx, wa):
    eye = jnp.eye(HEADS_PER_CHUNK, dtype=wx.dtype)

    def bd(w):
        w = w.reshape(DEPTH, N_GATE_CHUNKS, HEADS_PER_CHUNK, LRU_BLOCK, LRU_BLOCK)
        w = jnp.einsum("dchij,hk->dchikj", w, eye)
        return w.reshape(DEPTH, N_GATE_CHUNKS, GATE_CHUNK, GATE_CHUNK)

    return jnp.concatenate([bd(wx), bd(wa)], axis=-1).astype(BF16)


def _run(h, p, G, TT, ffn_tm, moe_tm, moe_te, lru0, ca0, cb0, W):
    lru_s, ca_s, cb_s = [], [], []
    for layer in range(DEPTH):
        h, lru, ca, cb = _mixer(layer, G, TT, h, ca0[layer], cb0[layer], lru0[layer], W)
        if layer % 2 == 0:
            h = _ffn_dense(layer, ffn_tm, h, p, W)
        else:
            h = _ffn_moe(layer, moe_tm, moe_te, h, p, W)
        lru_s.append(lru)
        ca_s.append(ca)
        cb_s.append(cb)
    y = _final_norm(ffn_tm, h, W["final_norm"])
    return y, jnp.stack(lru_s), jnp.stack(ca_s), jnp.stack(cb_s)


def kernel(x_prompt, x_sample, state_lru_h, state_conv_a, state_conv_b, p_prompt, p_sample,
           norm_mix, w_in, conv_a_w, conv_a_b, lru_wx, lru_bx, lru_wa, lru_ba, lru_lambda,
           w_a_out, conv_b_w, w_b_out, w_o, norm_ffn, ffn_w_gu, ffn_w_dn, router,
           moe_w_gu, moe_w_dn, norm_ple, w_ple_gate, w_ple_proj, final_norm):
    B, T, D = x_prompt.shape
    S = x_sample.shape[0]
    KA, KB = CONV_A_WIDTH - 1, CONV_B_WIDTH - 1

    def vec(v):
        return v.reshape(v.shape[0], 1, D)

    W = {
        "norm_mix": vec(norm_mix), "w_in": w_in.astype(BF16),
        "conv_a_w": conv_a_w, "conv_a_b": vec(conv_a_b),
        "w_gate": _block_diag_gates(lru_wx, lru_wa),
        "lru_bx": vec(lru_bx), "lru_ba": vec(lru_ba), "lru_lambda": vec(lru_lambda),
        "w_a_out": w_a_out.astype(BF16), "conv_b_w": conv_b_w, "w_b_out": w_b_out.astype(BF16),
        "w_o": w_o.astype(BF16), "norm_ffn": vec(norm_ffn),
        "ffn_w_gu": ffn_w_gu.astype(BF16), "ffn_w_dn": ffn_w_dn.astype(BF16),
        "router": jnp.pad(router, ((0, 0), (0, 0), (0, LANES - N_EXPERTS))).astype(BF16),
        "moe_w_gu": moe_w_gu.astype(BF16), "moe_w_dn": moe_w_dn.astype(BF16),
        "norm_ple": vec(norm_ple), "w_ple_gate": w_ple_gate.astype(BF16),
        "w_ple_proj": w_ple_proj.astype(BF16), "final_norm": final_norm.reshape(1, D),
    }

    hp = x_prompt.transpose(1, 0, 2).reshape(T * B, D)
    pp = p_prompt.transpose(0, 2, 1, 3).reshape(DEPTH, T * B, PLE_DIM).astype(BF16)
    zeros = lambda k: jnp.zeros((DEPTH, k * B, D), F32)
    y_p, lru_p, ca_p, cb_p = _run(hp, pp, B, 32, 1024, 512, 512, zeros(1), zeros(KA), zeros(KB), W)
    y_p = y_p.reshape(T, B, D).transpose(1, 0, 2)
    ca_p = ca_p.reshape(DEPTH, KA, B, D).transpose(0, 2, 1, 3)
    cb_p = cb_p.reshape(DEPTH, KB, B, D).transpose(0, 2, 1, 3)

    hs = x_sample.reshape(S, D)
    ps = p_sample.reshape(DEPTH, S, PLE_DIM).astype(BF16)
    ca0 = state_conv_a.transpose(0, 2, 1, 3).reshape(DEPTH, KA * S, D)
    cb0 = state_conv_b.transpose(0, 2, 1, 3).reshape(DEPTH, KB * S, D)
    y_s, lru_d, ca_d, cb_d = _run(hs, ps, S, 1, S, S, S, state_lru_h, ca0, cb0, W)
    y_s = y_s.reshape(S, 1, D)
    ca_d = ca_d.reshape(DEPTH, KA, S, D).transpose(0, 2, 1, 3)
    cb_d = cb_d.reshape(DEPTH, KB, S, D).transpose(0, 2, 1, 3)

    return (y_p, y_s, lru_p, ca_p, cb_p, lru_d, ca_d, cb_d)
```

```python
import functools

import jax
import jax.numpy as jnp
from jax import lax
from jax.experimental import pallas as pl
from jax.experimental.pallas import tpu as pltpu

D_MODEL = 1024
DEPTH = 4
LRU_HEADS = 16
LRU_BLOCK = D_MODEL // LRU_HEADS
CONV_A_WIDTH = 4
CONV_B_WIDTH = 3
LRU_C = 8.0
PLE_DIM = 256
D_FF = 3 * D_MODEL
N_EXPERTS = 8
D_FF_EXPERT = 3 * D_MODEL // 2
EPS = 1e-6
IN_COLS = 7 * D_MODEL

LANES = 128
GATE_CHUNK = 256
N_GATE_CHUNKS = D_MODEL // GATE_CHUNK
HEADS_PER_CHUNK = GATE_CHUNK // LRU_BLOCK
FF_TILE = 512
VMEM_LIMIT = 56 * 1024 * 1024

F32 = jnp.float32
BF16 = jnp.bfloat16


def _rms(x, g):
    var = jnp.mean(x * x, axis=-1, keepdims=True)
    return x * lax.rsqrt(var + EPS) * g


def _sigmoid(x):
    return 1.0 / (1.0 + jnp.exp(-x))


def _gelu_tanh(x):
    c = 0.7978845608028654
    return 0.5 * x * (1.0 + jnp.tanh(c * (x + 0.044715 * (x * x * x))))


def _dot(a, b):
    return jnp.dot(a, b, preferred_element_type=F32)


def _mixer_kernel(G, TT,
                  h_ref, ca0_ref, cb0_ref, l0_ref, gmix_ref, win_ref, caw_ref, cab_ref,
                  wg_ref, bx_ref, ba_ref, lam_ref, wao_ref, cbw_ref, wbo_ref, wo_ref,
                  hout_ref, lru_ref, ca_ref, cb_ref,
                  xa_buf, ub_buf, a_buf, b_buf, hstate):
    TM = TT * G
    HA = (CONV_A_WIDTH - 1) * G
    HB = (CONV_B_WIDTH - 1) * G
    D = D_MODEL

    @pl.when(pl.program_id(0) == 0)
    def _():
        xa_buf[0:HA, :] = ca0_ref[...]
        ub_buf[0:HB, :] = cb0_ref[...]
        hstate[...] = l0_ref[...]

    h = h_ref[...]
    n = _rms(h, gmix_ref[...]).astype(BF16)

    def proj(c):
        return _dot(n, win_ref[:, c * D:(c + 1) * D])

    xa = proj(0)
    xa_buf[HA:HA + TM, :] = xa
    ua = caw_ref[0:1, :] * xa_buf[0:TM, :]
    ua = ua + caw_ref[1:2, :] * xa_buf[G:G + TM, :]
    ua = ua + caw_ref[2:3, :] * xa_buf[2 * G:2 * G + TM, :]
    ua = ua + caw_ref[3:4, :] * xa
    ua = ua + cab_ref[...]
    ua_bf = ua.astype(BF16)

    lam = lam_ref[...]
    log_sig = -(jnp.maximum(-lam, 0.0) + jnp.log1p(jnp.exp(-jnp.abs(lam))))
    for c in range(N_GATE_CHUNKS):
        sl = slice(c * GATE_CHUNK, (c + 1) * GATE_CHUNK)
        g = _dot(ua_bf[:, sl], wg_ref[c])
        gate_x = _sigmoid(g[:, :GATE_CHUNK] + bx_ref[:, sl])
        gate_a = _sigmoid(g[:, GATE_CHUNK:] + ba_ref[:, sl])
        log_a = LRU_C * gate_a * log_sig[:, sl]
        a = jnp.exp(log_a)
        a_buf[:, sl] = a
        b_buf[:, sl] = jnp.sqrt(1.0 - a * a) * gate_x * ua[:, sl]

    def step(t, hc):
        r = pl.multiple_of(t * G, G)
        hn = a_buf[pl.ds(r, G), :] * hc + b_buf[pl.ds(r, G), :]
        b_buf[pl.ds(r, G), :] = hn
        return hn

    h_last = lax.fori_loop(0, TT, step, hstate[...], unroll=min(TT, 8))
    hstate[...] = h_last

    gated_a = (b_buf[...] * _gelu_tanh(proj(1))).astype(BF16)
    out_a = _dot(gated_a, wao_ref[...])

    ub_buf[HB:HB + TM, :] = proj(4) * proj(2)
    ub = cbw_ref[0:1, :] * ub_buf[0:TM, :]
    ub = ub + cbw_ref[1:2, :] * ub_buf[G:G + TM, :]
    ub = ub + cbw_ref[2:3, :] * ub_buf[2 * G:2 * G + TM, :]
    gated_b = (proj(3) * ub).astype(BF16)
    out_b = _dot(gated_b, wbo_ref[...])

    merged = _sigmoid(proj(5)) * out_a + _sigmoid(proj(6)) * out_b
    hout_ref[...] = h + _dot(merged.astype(BF16), wo_ref[...])

    new_ca = xa_buf[TM:TM + HA, :]
    new_cb = ub_buf[TM:TM + HB, :]
    xa_buf[0:HA, :] = new_ca
    ub_buf[0:HB, :] = new_cb
    ca_ref[...] = new_ca
    cb_ref[...] = new_cb
    lru_ref[...] = h_last


def _const_spec(shape, layer=None):
    if layer is None:
        return pl.BlockSpec(shape, lambda *_: (0,) * len(shape), pipeline_mode=pl.Buffered(1))
    return pl.BlockSpec((None,) + shape, lambda *_: (layer,) + (0,) * len(shape),
                        pipeline_mode=pl.Buffered(1))


def _mixer(layer, G, TT, h, ca0, cb0, l0, W):
    N = h.shape[0]
    TM = TT * G
    D = D_MODEL
    HA = (CONV_A_WIDTH - 1) * G
    HB = (CONV_B_WIDTH - 1) * G
    row = pl.BlockSpec((TM, D), lambda i: (i, 0))
    vec = _const_spec((1, D), layer)
    in_specs = [
        row,
        _const_spec((HA, D)), _const_spec((HB, D)), _const_spec((G, D)),
        vec,
        _const_spec((D, IN_COLS), layer),
        _const_spec((CONV_A_WIDTH, D), layer), vec,
        _const_spec((N_GATE_CHUNKS, GATE_CHUNK, 2 * GATE_CHUNK), layer),
        vec, vec, vec,
        _const_spec((D, D), layer),
        _const_spec((CONV_B_WIDTH, D), layer),
        _const_spec((D, D), layer),
        _const_spec((D, D), layer),
    ]
    out_specs = [row, _const_spec((G, D)), _const_spec((HA, D)), _const_spec((HB, D))]
    out_shape = [jax.ShapeDtypeStruct((N, D), F32), jax.ShapeDtypeStruct((G, D), F32),
                 jax.ShapeDtypeStruct((HA, D), F32), jax.ShapeDtypeStruct((HB, D), F32)]
    scratch = [pltpu.VMEM((HA + TM, D), F32), pltpu.VMEM((HB + TM, D), F32),
               pltpu.VMEM((TM, D), F32), pltpu.VMEM((TM, D), F32), pltpu.VMEM((G, D), F32)]
    return pl.pallas_call(
        functools.partial(_mixer_kernel, G, TT),
        grid=(N // TM,),
        in_specs=in_specs, out_specs=out_specs, out_shape=out_shape,
        scratch_shapes=scratch,
        compiler_params=pltpu.CompilerParams(dimension_semantics=("arbitrary",),
                                             vmem_limit_bytes=VMEM_LIMIT),
        name=f"mixer_g{G}",
    )(h, ca0, cb0, l0, W["norm_mix"], W["w_in"], W["conv_a_w"], W["conv_a_b"], W["w_gate"],
      W["lru_bx"], W["lru_ba"], W["lru_lambda"], W["w_a_out"], W["conv_b_w"], W["w_b_out"], W["w_o"])


def _ple(h1, gple_ref, wpg_ref, p_ref, wpp_ref):
    n3 = _rms(h1, gple_ref[...]).astype(BF16)
    gate = _sigmoid(_dot(n3, wpg_ref[...]))
    return h1 + gate * _dot(p_ref[...], wpp_ref[...])


def _ple_specs(layer, TM, ngrid):
    if ngrid == 1:
        p_spec = pl.BlockSpec((None, TM, PLE_DIM), lambda i: (layer, i, 0))
    else:
        p_spec = pl.BlockSpec((None, TM, PLE_DIM), lambda i, k: (layer, i, 0))
    return [_const_spec((1, D_MODEL), layer), _const_spec((D_MODEL, D_MODEL), layer), p_spec,
            _const_spec((PLE_DIM, D_MODEL), layer)]


def _ple_args(p, W):
    return [W["norm_ple"], W["w_ple_gate"], p, W["w_ple_proj"]]


def _ffn_kernel(nk, h_ref, gffn_ref, wg_ref, wu_ref, wd_ref, gple_ref, wpg_ref, p_ref, wpp_ref,
                hout_ref, n2_buf, acc):
    k = pl.program_id(1)

    @pl.when(k == 0)
    def _():
        n2_buf[...] = _rms(h_ref[...], gffn_ref[...]).astype(BF16)
        acc[...] = jnp.zeros_like(acc)

    x = n2_buf[...]
    g = _dot(x, wg_ref[...])
    u = _dot(x, wu_ref[...])
    acc[...] += _dot((g * _sigmoid(g) * u).astype(BF16), wd_ref[...])

    @pl.when(k == nk - 1)
    def _():
        hout_ref[...] = _ple(h_ref[...] + acc[...], gple_ref, wpg_ref, p_ref, wpp_ref)


def _ffn_dense(layer, TM, h, p, W):
    N = h.shape[0]
    D = D_MODEL
    j = layer // 2
    nk = D_FF // FF_TILE
    row = pl.BlockSpec((TM, D), lambda i, k: (i, 0))
    in_specs = [row, _const_spec((1, D), layer),
                pl.BlockSpec((None, D, FF_TILE), lambda i, k: (j, 0, k)),
                pl.BlockSpec((None, D, FF_TILE), lambda i, k: (j, 0, nk + k)),
                pl.BlockSpec((None, FF_TILE, D), lambda i, k: (j, k, 0))] + _ple_specs(layer, TM, 2)
    return pl.pallas_call(
        functools.partial(_ffn_kernel, nk),
        grid=(N // TM, nk),
        in_specs=in_specs, out_specs=row,
        out_shape=jax.ShapeDtypeStruct((N, D), F32),
        scratch_shapes=[pltpu.VMEM((TM, D), BF16), pltpu.VMEM((TM, D), F32)],
        compiler_params=pltpu.CompilerParams(dimension_semantics=("parallel", "arbitrary"),
                                             vmem_limit_bytes=VMEM_LIMIT),
        name=f"ffn_dense_{TM}",
    )(h, W["norm_ffn"], W["ffn_w_gu"], W["ffn_w_gu"], W["ffn_w_dn"], *_ple_args(p, W))


def _route_kernel(h_ref, gffn_ref, router_ref, n2_ref, idx_ref, wts_ref, cnt_ref, carry):
    TM = h_ref.shape[0]

    @pl.when(pl.program_id(0) == 0)
    def _():
        carry[...] = jnp.zeros_like(carry)

    n2 = _rms(h_ref[...], gffn_ref[...])
    n2_ref[...] = n2
    lane = lax.broadcasted_iota(jnp.int32, (TM, LANES), 1)
    logits = jnp.where(lane < N_EXPERTS, _dot(n2.astype(BF16), router_ref[...]), -jnp.inf)
    v1 = jnp.max(logits, axis=-1, keepdims=True)
    i1 = jnp.min(jnp.where(logits == v1, lane, LANES), axis=-1, keepdims=True)
    rest = jnp.where(lane == i1, -jnp.inf, logits)
    v2 = jnp.max(rest, axis=-1, keepdims=True)
    i2 = jnp.min(jnp.where(rest == v2, lane, LANES), axis=-1, keepdims=True)
    ex = jnp.exp(v2 - v1)
    w1 = 1.0 / (1.0 + ex)
    w2 = ex / (1.0 + ex)
    m1 = lane == i1
    m2 = lane == i2
    chosen = jnp.where(m1 | m2, 1.0, 0.0)
    before = (lax.broadcasted_iota(jnp.int32, (TM, TM), 1)
              < lax.broadcasted_iota(jnp.int32, (TM, TM), 0))
    counts = _dot(jnp.where(before, 1.0, 0.0).astype(BF16), chosen.astype(BF16)) + carry[...]
    r1 = jnp.sum(jnp.where(m1, counts, 0.0), axis=-1, keepdims=True).astype(jnp.int32)
    r2 = jnp.sum(jnp.where(m2, counts, 0.0), axis=-1, keepdims=True).astype(jnp.int32)
    carry[...] += jnp.sum(chosen, axis=0, keepdims=True)
    cnt_ref[...] = carry[...].astype(jnp.int32)
    idx_ref[...] = jnp.where(lane == 0, i1, jnp.where(lane == 1, i2,
                             jnp.where(lane == 2, r1, jnp.where(lane == 3, r2, 0))))
    wts_ref[...] = jnp.where(lane == 0, w1, jnp.where(lane == 1, w2, 0.0))


def _route(layer, TM, h, W):
    N = h.shape[0]
    D = D_MODEL
    row = pl.BlockSpec((TM, D), lambda i: (i, 0))
    info = pl.BlockSpec((TM, LANES), lambda i: (i, 0))
    return pl.pallas_call(
        _route_kernel, grid=(N // TM,),
        in_specs=[row, _const_spec((1, D), layer), _const_spec((D, LANES), layer // 2)],
        out_specs=[row, info, info, _const_spec((1, LANES))],
        out_shape=[jax.ShapeDtypeStruct((N, D), F32), jax.ShapeDtypeStruct((N, LANES), jnp.int32),
                   jax.ShapeDtypeStruct((N, LANES), F32), jax.ShapeDtypeStruct((1, LANES), jnp.int32)],
        scratch_shapes=[pltpu.VMEM((1, LANES), F32)],
        compiler_params=pltpu.CompilerParams(dimension_semantics=("arbitrary",)),
        name=f"moe_route_{TM}",
    )(h, W["norm_ffn"], W["router"])


def _row_copy(src, s, dst, d, sem):
    return pltpu.make_async_copy(src.at[pl.ds(s, 1)], dst.at[pl.ds(d, 1)], sem)


def _dispatch_kernel(TM, TE, n_tiles, cnt_ref, off_ref, pos_ref, n2_ref, xs_hbm, zero_buf, sem, pad_sem):
    i = pl.program_id(0)

    @pl.when(i == 0)
    def _():
        zero_buf[...] = jnp.zeros_like(zero_buf)
        for e in range(N_EXPERTS):
            lo = off_ref[e] + cnt_ref[e]
            hi = off_ref[e + 1]

            def pad(r, c):
                _row_copy(zero_buf, 0, xs_hbm, r, pad_sem).start()
                return c

            def pad_wait(r, c):
                _row_copy(zero_buf, 0, xs_hbm, 0, pad_sem).wait()
                return c

            lax.fori_loop(lo, hi, pad, 0)
            lax.fori_loop(lo, hi, pad_wait, 0)

        def tile_copy(t):
            return pltpu.make_async_copy(zero_buf, xs_hbm.at[pl.ds(pl.multiple_of(t * TE, TE), TE)], pad_sem)

        def fill(t, c):
            tile_copy(t).start()
            return c

        def fill_wait(t, c):
            tile_copy(t).wait()
            return c

        first_unused = off_ref[N_EXPERTS] // TE
        lax.fori_loop(first_unused, n_tiles, fill, 0)
        lax.fori_loop(first_unused, n_tiles, fill_wait, 0)

    def issue(r, c):
        _row_copy(n2_ref, r, xs_hbm, pos_ref[0, 0, 2 * r], sem).start()
        _row_copy(n2_ref, r, xs_hbm, pos_ref[0, 0, 2 * r + 1], sem).start()
        return c

    def drain(r, c):
        _row_copy(n2_ref, 0, xs_hbm, 0, sem).wait()
        _row_copy(n2_ref, 0, xs_hbm, 0, sem).wait()
        return c

    lax.fori_loop(0, TM, issue, 0, unroll=8)
    lax.fori_loop(0, TM, drain, 0, unroll=8)


def _dispatch(TM, TE, n_tiles, cnt, off, pos, n2):
    N, D = n2.shape
    return pl.pallas_call(
        functools.partial(_dispatch_kernel, TM, TE, n_tiles),
        grid_spec=pltpu.PrefetchScalarGridSpec(
            num_scalar_prefetch=2, grid=(N // TM,),
            in_specs=[pl.BlockSpec((1, 1, 2 * TM), lambda i, c, o: (i, 0, 0), memory_space=pltpu.SMEM),
                      pl.BlockSpec((TM, D), lambda i, c, o: (i, 0))],
            out_specs=pl.BlockSpec(memory_space=pl.ANY),
            scratch_shapes=[pltpu.VMEM((TE, D), F32), pltpu.SemaphoreType.DMA, pltpu.SemaphoreType.DMA]),
        out_shape=jax.ShapeDtypeStruct((n_tiles * TE, D), F32),
        compiler_params=pltpu.CompilerParams(dimension_semantics=("arbitrary",),
                                             disable_bounds_checks=True),
        name=f"moe_dispatch_{TM}",
    )(cnt, off, pos, n2)


def _expert_kernel(te_ref, na_ref, x_ref, wgu_ref, wdn_ref, y_ref):
    used = pl.program_id(0) < na_ref[0]

    @pl.when(used)
    def _():
        gu = _dot(x_ref[...].astype(BF16), wgu_ref[...])
        g = gu[:, :D_FF_EXPERT]
        u = gu[:, D_FF_EXPERT:]
        y_ref[...] = _dot((g * _sigmoid(g) * u).astype(BF16), wdn_ref[...])

    @pl.when(jnp.logical_not(used))
    def _():
        y_ref[...] = jnp.zeros_like(y_ref)


def _experts(layer, TE, te, na, xs, W):
    P, D = xs.shape
    j = layer // 2
    rows = pl.BlockSpec((TE, D), lambda t, te, na: (t, 0))
    return pl.pallas_call(
        _expert_kernel,
        grid_spec=pltpu.PrefetchScalarGridSpec(
            num_scalar_prefetch=2, grid=(P // TE,),
            in_specs=[rows,
                      pl.BlockSpec((None, None, D, 2 * D_FF_EXPERT), lambda t, te, na: (j, te[t], 0, 0)),
                      pl.BlockSpec((None, None, D_FF_EXPERT, D), lambda t, te, na: (j, te[t], 0, 0))],
            out_specs=rows),
        out_shape=jax.ShapeDtypeStruct((P, D), F32),
        compiler_params=pltpu.CompilerParams(dimension_semantics=("arbitrary",),
                                             vmem_limit_bytes=VMEM_LIMIT),
        name=f"moe_experts_{TE}",
    )(te, na, xs, W["moe_w_gu"], W["moe_w_dn"])


def _combine_kernel(TM, pos_ref, h_ref, wts_ref, ys_hbm, gple_ref, wpg_ref, p_ref, wpp_ref,
                    hout_ref, y_buf, sem):
    def issue(r, c):
        _row_copy(ys_hbm, pos_ref[0, 0, 2 * r], y_buf.at[0], r, sem).start()
        _row_copy(ys_hbm, pos_ref[0, 0, 2 * r + 1], y_buf.at[1], r, sem).start()
        return c

    def drain(r, c):
        _row_copy(ys_hbm, 0, y_buf.at[0], 0, sem).wait()
        _row_copy(ys_hbm, 0, y_buf.at[1], 0, sem).wait()
        return c

    lax.fori_loop(0, TM, issue, 0, unroll=8)
    lax.fori_loop(0, TM, drain, 0, unroll=8)
    moe = wts_ref[:, 0:1] * y_buf[0] + wts_ref[:, 1:2] * y_buf[1]
    hout_ref[...] = _ple(h_ref[...] + moe, gple_ref, wpg_ref, p_ref, wpp_ref)


def _combine(layer, TM, pos, h, wts, ys, p, W):
    N, D = h.shape
    row = pl.BlockSpec((TM, D), lambda i: (i, 0))
    in_specs = [pl.BlockSpec((1, 1, 2 * TM), lambda i: (i, 0, 0), memory_space=pltpu.SMEM),
                row, pl.BlockSpec((TM, LANES), lambda i: (i, 0)),
                pl.BlockSpec(memory_space=pl.ANY)] + _ple_specs(layer, TM, 1)
    return pl.pallas_call(
        functools.partial(_combine_kernel, TM),
        grid=(N // TM,),
        in_specs=in_specs, out_specs=row,
        out_shape=jax.ShapeDtypeStruct((N, D), F32),
        scratch_shapes=[pltpu.VMEM((2, TM, D), F32), pltpu.SemaphoreType.DMA],
        compiler_params=pltpu.CompilerParams(dimension_semantics=("arbitrary",),
                                             vmem_limit_bytes=VMEM_LIMIT, disable_bounds_checks=True),
        name=f"moe_combine_{TM}",
    )(pos, h, wts, ys, *_ple_args(p, W))


def _ffn_moe(layer, TM, TE, h, p, W):
    N = h.shape[0]
    n_tiles = 2 * N // TE + N_EXPERTS
    n2, idx, wts, cnt = _route(layer, TM, h, W)
    cnt = cnt[0, :N_EXPERTS]
    ends = jnp.cumsum((cnt + TE - 1) // TE).astype(jnp.int32)
    off = jnp.concatenate([jnp.zeros((1,), jnp.int32), ends * TE])
    experts = jnp.arange(N_EXPERTS, dtype=jnp.int32)

    def first_row(e):
        return jnp.sum(jnp.where(e[:, None] == experts[None, :], off[None, :N_EXPERTS], 0), axis=1)

    pos = jnp.stack([first_row(idx[:, 0]) + idx[:, 2], first_row(idx[:, 1]) + idx[:, 3]], axis=1)
    pos = pos.reshape(N // TM, 1, 2 * TM)
    t = jnp.minimum(jnp.arange(n_tiles, dtype=jnp.int32), ends[-1] - 1)
    te = jnp.sum(t[:, None] >= ends[None, :], axis=1).astype(jnp.int32)
    xs = _dispatch(TM, TE, n_tiles, cnt, off, pos, n2)
    ys = _experts(layer, TE, te, ends[-1:], xs, W)
    return _combine(layer, TM, pos, h, wts, ys, p, W)


def _norm_kernel(h_ref, g_ref, o_ref):
    o_ref[...] = _rms(h_ref[...], g_ref[...])


def _final_norm(TM, h, g):
    N = h.shape[0]
    row = pl.BlockSpec((TM, D_MODEL), lambda i: (i, 0))
    return pl.pallas_call(
        _norm_kernel, grid=(N // TM,),
        in_specs=[row, _const_spec((1, D_MODEL))], out_specs=row,
        out_shape=jax.ShapeDtypeStruct((N, D_MODEL), F32),
        compiler_params=pltpu.CompilerParams(dimension_semantics=("parallel",)),
        name=f"final_norm_{TM}",
    )(h, g)


def _block_diag_gates(wx, wa):
    eye = jnp.eye(HEADS_PER_CHUNK, dtype=wx.dtype)

    def bd(w):
        w = w.reshape(DEPTH, N_GATE_CHUNKS, HEADS_PER_CHUNK, LRU_BLOCK, LRU_BLOCK)
        w = jnp.einsum("dchij,hk->dchikj", w, eye)
        return w.reshape(DEPTH, N_GATE_CHUNKS, GATE_CHUNK, GATE_CHUNK)

    return jnp.concatenate([bd(wx), bd(wa)], axis=-1).astype(BF16)


def _run(h, p, G, TT, ffn_tm, moe_tm, moe_te, lru0, ca0, cb0, W):
    lru_s, ca_s, cb_s = [], [], []
    for layer in range(DEPTH):
        h, lru, ca, cb = _mixer(layer, G, TT, h, ca0[layer], cb0[layer], lru0[layer], W)
        if layer % 2 == 0:
            h = _ffn_dense(layer, ffn_tm, h, p, W)
        else:
            h = _ffn_moe(layer, moe_tm, moe_te, h, p, W)
        lru_s.append(lru)
        ca_s.append(ca)
        cb_s.append(cb)
    y = _final_norm(ffn_tm, h, W["final_norm"])
    return y, jnp.stack(lru_s), jnp.stack(ca_s), jnp.stack(cb_s)


def kernel(x_prompt, x_sample, state_lru_h, state_conv_a, state_conv_b, p_prompt, p_sample,
           norm_mix, w_in, conv_a_w, conv_a_b, lru_wx, lru_bx, lru_wa, lru_ba, lru_lambda,
           w_a_out, conv_b_w, w_b_out, w_o, norm_ffn, ffn_w_gu, ffn_w_dn, router,
           moe_w_gu, moe_w_dn, norm_ple, w_ple_gate, w_ple_proj, final_norm):
    B, T, D = x_prompt.shape
    S = x_sample.shape[0]
    KA, KB = CONV_A_WIDTH - 1, CONV_B_WIDTH - 1

    def vec(v):
        return v.reshape(v.shape[0], 1, D)

    W = {
        "norm_mix": vec(norm_mix), "w_in": w_in.astype(BF16),
        "conv_a_w": conv_a_w, "conv_a_b": vec(conv_a_b),
        "w_gate": _block_diag_gates(lru_wx, lru_wa),
        "lru_bx": vec(lru_bx), "lru_ba": vec(lru_ba), "lru_lambda": vec(lru_lambda),
        "w_a_out": w_a_out.astype(BF16), "conv_b_w": conv_b_w, "w_b_out": w_b_out.astype(BF16),
        "w_o": w_o.astype(BF16), "norm_ffn": vec(norm_ffn),
        "ffn_w_gu": ffn_w_gu.astype(BF16), "ffn_w_dn": ffn_w_dn.astype(BF16),
        "router": jnp.pad(router, ((0, 0), (0, 0), (0, LANES - N_EXPERTS))).astype(BF16),
        "moe_w_gu": moe_w_gu.astype(BF16), "moe_w_dn": moe_w_dn.astype(BF16),
        "norm_ple": vec(norm_ple), "w_ple_gate": w_ple_gate.astype(BF16),
        "w_ple_proj": w_ple_proj.astype(BF16), "final_norm": final_norm.reshape(1, D),
    }

    hp = x_prompt.transpose(1, 0, 2).reshape(T * B, D)
    pp = p_prompt.transpose(0, 2, 1, 3).reshape(DEPTH, T * B, PLE_DIM).astype(BF16)
    zeros = lambda k: jnp.zeros((DEPTH, k * B, D), F32)
    y_p, lru_p, ca_p, cb_p = _run(hp, pp, B, 64, 1024, 512, 512, zeros(1), zeros(KA), zeros(KB), W)
    y_p = y_p.reshape(T, B, D).transpose(1, 0, 2)
    ca_p = ca_p.reshape(DEPTH, KA, B, D).transpose(0, 2, 1, 3)
    cb_p = cb_p.reshape(DEPTH, KB, B, D).transpose(0, 2, 1, 3)

    hs = x_sample.reshape(S, D)
    ps = p_sample.reshape(DEPTH, S, PLE_DIM).astype(BF16)
    ca0 = state_conv_a.transpose(0, 2, 1, 3).reshape(DEPTH, KA * S, D)
    cb0 = state_conv_b.transpose(0, 2, 1, 3).reshape(DEPTH, KB * S, D)
    y_s, lru_d, ca_d, cb_d = _run(hs, ps, S, 1, S, S, S, state_lru_h, ca0, cb0, W)
    y_s = y_s.reshape(S, 1, D)
    ca_d = ca_d.reshape(DEPTH, KA, S, D).transpose(0, 2, 1, 3)
    cb_d = cb_d.reshape(DEPTH, KB, S, D).transpose(0, 2, 1, 3)

    return (y_p, y_s, lru_p, ca_p, cb_p, lru_d, ca_d, cb_d)
```

```python
import functools

import jax
import jax.numpy as jnp
from jax import lax
from jax.experimental import pallas as pl
from jax.experimental.pallas import tpu as pltpu

D_MODEL = 1024
DEPTH = 4
LRU_HEADS = 16
LRU_BLOCK = D_MODEL // LRU_HEADS
CONV_A_WIDTH = 4
CONV_B_WIDTH = 3
LRU_C = 8.0
PLE_DIM = 256
D_FF = 3 * D_MODEL
N_EXPERTS = 8
D_FF_EXPERT = 3 * D_MODEL // 2
EPS = 1e-6
IN_COLS = 7 * D_MODEL

LANES = 128
GATE_CHUNK = 256
N_GATE_CHUNKS = D_MODEL // GATE_CHUNK
HEADS_PER_CHUNK = GATE_CHUNK // LRU_BLOCK
FF_TILE = 1024
VMEM_LIMIT = 56 * 1024 * 1024

F32 = jnp.float32
BF16 = jnp.bfloat16


def _rms(x, g):
    var = jnp.mean(x * x, axis=-1, keepdims=True)
    return x * lax.rsqrt(var + EPS) * g


def _sigmoid(x):
    return 1.0 / (1.0 + jnp.exp(-x))


def _gelu_tanh(x):
    c = 0.7978845608028654
    return 0.5 * x * (1.0 + jnp.tanh(c * (x + 0.044715 * (x * x * x))))


def _dot(a, b):
    return jnp.dot(a, b, preferred_element_type=F32)


def _mixer_kernel(G, TT,
                  h_ref, ca0_ref, cb0_ref, l0_ref, gmix_ref, win_ref, caw_ref, cab_ref,
                  wg_ref, bx_ref, ba_ref, lam_ref, wao_ref, cbw_ref, wbo_ref, wo_ref,
                  hout_ref, lru_ref, ca_ref, cb_ref,
                  xa_buf, ub_buf, a_buf, b_buf, hstate):
    TM = TT * G
    HA = (CONV_A_WIDTH - 1) * G
    HB = (CONV_B_WIDTH - 1) * G
    D = D_MODEL

    @pl.when(pl.program_id(0) == 0)
    def _():
        xa_buf[0:HA, :] = ca0_ref[...]
        ub_buf[0:HB, :] = cb0_ref[...]
        hstate[...] = l0_ref[...]

    h = h_ref[...]
    n = _rms(h, gmix_ref[...]).astype(BF16)

    def proj(c):
        return _dot(n, win_ref[:, c * D:(c + 1) * D])

    xa = proj(0)
    xa_buf[HA:HA + TM, :] = xa
    ua = caw_ref[0:1, :] * xa_buf[0:TM, :]
    ua = ua + caw_ref[1:2, :] * xa_buf[G:G + TM, :]
    ua = ua + caw_ref[2:3, :] * xa_buf[2 * G:2 * G + TM, :]
    ua = ua + caw_ref[3:4, :] * xa
    ua = ua + cab_ref[...]
    ua_bf = ua.astype(BF16)

    lam = lam_ref[...]
    log_sig = -(jnp.maximum(-lam, 0.0) + jnp.log1p(jnp.exp(-jnp.abs(lam))))
    for c in range(N_GATE_CHUNKS):
        sl = slice(c * GATE_CHUNK, (c + 1) * GATE_CHUNK)
        g = _dot(ua_bf[:, sl], wg_ref[c])
        gate_x = _sigmoid(g[:, :GATE_CHUNK] + bx_ref[:, sl])
        gate_a = _sigmoid(g[:, GATE_CHUNK:] + ba_ref[:, sl])
        log_a = LRU_C * gate_a * log_sig[:, sl]
        a = jnp.exp(log_a)
        a_buf[:, sl] = a
        b_buf[:, sl] = jnp.sqrt(1.0 - a * a) * gate_x * ua[:, sl]

    def step(t, hc):
        r = pl.multiple_of(t * G, G)
        hn = a_buf[pl.ds(r, G), :] * hc + b_buf[pl.ds(r, G), :]
        b_buf[pl.ds(r, G), :] = hn
        return hn

    h_last = lax.fori_loop(0, TT, step, hstate[...], unroll=min(TT, 8))
    hstate[...] = h_last

    gated_a = (b_buf[...] * _gelu_tanh(proj(1))).astype(BF16)
    out_a = _dot(gated_a, wao_ref[...])

    ub_buf[HB:HB + TM, :] = proj(4) * proj(2)
    ub = cbw_ref[0:1, :] * ub_buf[0:TM, :]
    ub = ub + cbw_ref[1:2, :] * ub_buf[G:G + TM, :]
    ub = ub + cbw_ref[2:3, :] * ub_buf[2 * G:2 * G + TM, :]
    gated_b = (proj(3) * ub).astype(BF16)
    out_b = _dot(gated_b, wbo_ref[...])

    merged = _sigmoid(proj(5)) * out_a + _sigmoid(proj(6)) * out_b
    hout_ref[...] = h + _dot(merged.astype(BF16), wo_ref[...])

    new_ca = xa_buf[TM:TM + HA, :]
    new_cb = ub_buf[TM:TM + HB, :]
    xa_buf[0:HA, :] = new_ca
    ub_buf[0:HB, :] = new_cb
    ca_ref[...] = new_ca
    cb_ref[...] = new_cb
    lru_ref[...] = h_last


def _const_spec(shape, layer=None):
    if layer is None:
        return pl.BlockSpec(shape, lambda *_: (0,) * len(shape), pipeline_mode=pl.Buffered(1))
    return pl.BlockSpec((None,) + shape, lambda *_: (layer,) + (0,) * len(shape),
                        pipeline_mode=pl.Buffered(1))


def _mixer(layer, G, TT, h, ca0, cb0, l0, W):
    N = h.shape[0]
    TM = TT * G
    D = D_MODEL
    HA = (CONV_A_WIDTH - 1) * G
    HB = (CONV_B_WIDTH - 1) * G
    row = pl.BlockSpec((TM, D), lambda i: (i, 0))
    vec = _const_spec((1, D), layer)
    in_specs = [
        row,
        _const_spec((HA, D)), _const_spec((HB, D)), _const_spec((G, D)),
        vec,
        _const_spec((D, IN_COLS), layer),
        _const_spec((CONV_A_WIDTH, D), layer), vec,
        _const_spec((N_GATE_CHUNKS, GATE_CHUNK, 2 * GATE_CHUNK), layer),
        vec, vec, vec,
        _const_spec((D, D), layer),
        _const_spec((CONV_B_WIDTH, D), layer),
        _const_spec((D, D), layer),
        _const_spec((D, D), layer),
    ]
    out_specs = [row, _const_spec((G, D)), _const_spec((HA, D)), _const_spec((HB, D))]
    out_shape = [jax.ShapeDtypeStruct((N, D), F32), jax.ShapeDtypeStruct((G, D), F32),
                 jax.ShapeDtypeStruct((HA, D), F32), jax.ShapeDtypeStruct((HB, D), F32)]
    scratch = [pltpu.VMEM((HA + TM, D), F32), pltpu.VMEM((HB + TM, D), F32),
               pltpu.VMEM((TM, D), F32), pltpu.VMEM((TM, D), F32), pltpu.VMEM((G, D), F32)]
    return pl.pallas_call(
        functools.partial(_mixer_kernel, G, TT),
        grid=(N // TM,),
        in_specs=in_specs, out_specs=out_specs, out_shape=out_shape,
        scratch_shapes=scratch,
        compiler_params=pltpu.CompilerParams(dimension_semantics=("arbitrary",),
                                             vmem_limit_bytes=VMEM_LIMIT),
        name=f"mixer_g{G}",
    )(h, ca0, cb0, l0, W["norm_mix"], W["w_in"], W["conv_a_w"], W["conv_a_b"], W["w_gate"],
      W["lru_bx"], W["lru_ba"], W["lru_lambda"], W["w_a_out"], W["conv_b_w"], W["w_b_out"], W["w_o"])


def _ple(h1, gple_ref, wpg_ref, p_ref, wpp_ref):
    n3 = _rms(h1, gple_ref[...]).astype(BF16)
    gate = _sigmoid(_dot(n3, wpg_ref[...]))
    return h1 + gate * _dot(p_ref[...], wpp_ref[...])


def _ple_specs(layer, TM, ngrid):
    if ngrid == 1:
        p_spec = pl.BlockSpec((None, TM, PLE_DIM), lambda i: (layer, i, 0))
    else:
        p_spec = pl.BlockSpec((None, TM, PLE_DIM), lambda i, k: (layer, i, 0))
    return [_const_spec((1, D_MODEL), layer), _const_spec((D_MODEL, D_MODEL), layer), p_spec,
            _const_spec((PLE_DIM, D_MODEL), layer)]


def _ple_args(p, W):
    return [W["norm_ple"], W["w_ple_gate"], p, W["w_ple_proj"]]


def _ffn_kernel(nk, h_ref, gffn_ref, wg_ref, wu_ref, wd_ref, gple_ref, wpg_ref, p_ref, wpp_ref,
                hout_ref, n2_buf, acc):
    k = pl.program_id(1)

    @pl.when(k == 0)
    def _():
        n2_buf[...] = _rms(h_ref[...], gffn_ref[...]).astype(BF16)
        acc[...] = jnp.zeros_like(acc)

    x = n2_buf[...]
    g = _dot(x, wg_ref[...])
    u = _dot(x, wu_ref[...])
    acc[...] += _dot((g * _sigmoid(g) * u).astype(BF16), wd_ref[...])

    @pl.when(k == nk - 1)
    def _():
        hout_ref[...] = _ple(h_ref[...] + acc[...], gple_ref, wpg_ref, p_ref, wpp_ref)


def _ffn_dense(layer, TM, h, p, W):
    N = h.shape[0]
    D = D_MODEL
    j = layer // 2
    nk = D_FF // FF_TILE
    row = pl.BlockSpec((TM, D), lambda i, k: (i, 0))
    in_specs = [row, _const_spec((1, D), layer),
                pl.BlockSpec((None, D, FF_TILE), lambda i, k: (j, 0, k)),
                pl.BlockSpec((None, D, FF_TILE), lambda i, k: (j, 0, nk + k)),
                pl.BlockSpec((None, FF_TILE, D), lambda i, k: (j, k, 0))] + _ple_specs(layer, TM, 2)
    return pl.pallas_call(
        functools.partial(_ffn_kernel, nk),
        grid=(N // TM, nk),
        in_specs=in_specs, out_specs=row,
        out_shape=jax.ShapeDtypeStruct((N, D), F32),
        scratch_shapes=[pltpu.VMEM((TM, D), BF16), pltpu.VMEM((TM, D), F32)],
        compiler_params=pltpu.CompilerParams(dimension_semantics=("parallel", "arbitrary"),
                                             vmem_limit_bytes=VMEM_LIMIT),
        name=f"ffn_dense_{TM}",
    )(h, W["norm_ffn"], W["ffn_w_gu"], W["ffn_w_gu"], W["ffn_w_dn"], *_ple_args(p, W))


def _route_kernel(h_ref, gffn_ref, router_ref, n2_ref, idx_ref, wts_ref, cnt_ref, carry):
    TM = h_ref.shape[0]

    @pl.when(pl.program_id(0) == 0)
    def _():
        carry[...] = jnp.zeros_like(carry)

    n2 = _rms(h_ref[...], gffn_ref[...])
    n2_ref[...] = n2
    lane = lax.broadcasted_iota(jnp.int32, (TM, LANES), 1)
    logits = jnp.where(lane < N_EXPERTS, _dot(n2.astype(BF16), router_ref[...]), -jnp.inf)
    v1 = jnp.max(logits, axis=-1, keepdims=True)
    i1 = jnp.min(jnp.where(logits == v1, lane, LANES), axis=-1, keepdims=True)
    rest = jnp.where(lane == i1, -jnp.inf, logits)
    v2 = jnp.max(rest, axis=-1, keepdims=True)
    i2 = jnp.min(jnp.where(rest == v2, lane, LANES), axis=-1, keepdims=True)
    ex = jnp.exp(v2 - v1)
    w1 = 1.0 / (1.0 + ex)
    w2 = ex / (1.0 + ex)
    m1 = lane == i1
    m2 = lane == i2
    chosen = jnp.where(m1 | m2, 1.0, 0.0)
    before = (lax.broadcasted_iota(jnp.int32, (TM, TM), 1)
              < lax.broadcasted_iota(jnp.int32, (TM, TM), 0))
    counts = _dot(jnp.where(before, 1.0, 0.0).astype(BF16), chosen.astype(BF16)) + carry[...]
    r1 = jnp.sum(jnp.where(m1, counts, 0.0), axis=-1, keepdims=True).astype(jnp.int32)
    r2 = jnp.sum(jnp.where(m2, counts, 0.0), axis=-1, keepdims=True).astype(jnp.int32)
    carry[...] += jnp.sum(chosen, axis=0, keepdims=True)
    cnt_ref[...] = carry[...].astype(jnp.int32)
    idx_ref[...] = jnp.where(lane == 0, i1, jnp.where(lane == 1, i2,
                             jnp.where(lane == 2, r1, jnp.where(lane == 3, r2, 0))))
    wts_ref[...] = jnp.where(lane == 0, w1, jnp.where(lane == 1, w2, 0.0))


def _route(layer, TM, h, W):
    N = h.shape[0]
    D = D_MODEL
    row = pl.BlockSpec((TM, D), lambda i: (i, 0))
    info = pl.BlockSpec((TM, LANES), lambda i: (i, 0))
    return pl.pallas_call(
        _route_kernel, grid=(N // TM,),
        in_specs=[row, _const_spec((1, D), layer), _const_spec((D, LANES), layer // 2)],
        out_specs=[row, info, info, _const_spec((1, LANES))],
        out_shape=[jax.ShapeDtypeStruct((N, D), F32), jax.ShapeDtypeStruct((N, LANES), jnp.int32),
                   jax.ShapeDtypeStruct((N, LANES), F32), jax.ShapeDtypeStruct((1, LANES), jnp.int32)],
        scratch_shapes=[pltpu.VMEM((1, LANES), F32)],
        compiler_params=pltpu.CompilerParams(dimension_semantics=("arbitrary",)),
        name=f"moe_route_{TM}",
    )(h, W["norm_ffn"], W["router"])


def _row_copy(src, s, dst, d, sem):
    return pltpu.make_async_copy(src.at[pl.ds(s, 1)], dst.at[pl.ds(d, 1)], sem)


def _dispatch_kernel(TM, TE, n_tiles, cnt_ref, off_ref, pos_ref, n2_ref, xs_hbm, zero_buf, sem, pad_sem):
    i = pl.program_id(0)

    @pl.when(i == 0)
    def _():
        zero_buf[...] = jnp.zeros_like(zero_buf)
        for e in range(N_EXPERTS):
            lo = off_ref[e] + cnt_ref[e]
            hi = off_ref[e + 1]

            def pad(r, c):
                _row_copy(zero_buf, 0, xs_hbm, r, pad_sem).start()
                return c

            def pad_wait(r, c):
                _row_copy(zero_buf, 0, xs_hbm, 0, pad_sem).wait()
                return c

            lax.fori_loop(lo, hi, pad, 0)
            lax.fori_loop(lo, hi, pad_wait, 0)

        def tile_copy(t):
            return pltpu.make_async_copy(zero_buf, xs_hbm.at[pl.ds(pl.multiple_of(t * TE, TE), TE)], pad_sem)

        def fill(t, c):
            tile_copy(t).start()
            return c

        def fill_wait(t, c):
            tile_copy(t).wait()
            return c

        first_unused = off_ref[N_EXPERTS] // TE
        lax.fori_loop(first_unused, n_tiles, fill, 0)
        lax.fori_loop(first_unused, n_tiles, fill_wait, 0)

    def issue(r, c):
        _row_copy(n2_ref, r, xs_hbm, pos_ref[0, 0, 2 * r], sem).start()
        _row_copy(n2_ref, r, xs_hbm, pos_ref[0, 0, 2 * r + 1], sem).start(priority=1)
        return c

    def drain(r, c):
        _row_copy(n2_ref, 0, xs_hbm, 0, sem).wait()
        _row_copy(n2_ref, 0, xs_hbm, 0, sem).wait()
        return c

    lax.fori_loop(0, TM, issue, 0, unroll=8)
    lax.fori_loop(0, TM, drain, 0, unroll=8)


def _dispatch(TM, TE, n_tiles, cnt, off, pos, n2):
    N, D = n2.shape
    return pl.pallas_call(
        functools.partial(_dispatch_kernel, TM, TE, n_tiles),
        grid_spec=pltpu.PrefetchScalarGridSpec(
            num_scalar_prefetch=2, grid=(N // TM,),
            in_specs=[pl.BlockSpec((1, 1, 2 * TM), lambda i, c, o: (i, 0, 0), memory_space=pltpu.SMEM),
                      pl.BlockSpec((TM, D), lambda i, c, o: (i, 0))],
            out_specs=pl.BlockSpec(memory_space=pl.ANY),
            scratch_shapes=[pltpu.VMEM((TE, D), F32), pltpu.SemaphoreType.DMA, pltpu.SemaphoreType.DMA]),
        out_shape=jax.ShapeDtypeStruct((n_tiles * TE, D), F32),
        compiler_params=pltpu.CompilerParams(dimension_semantics=("arbitrary",),
                                             disable_bounds_checks=True),
        name=f"moe_dispatch_{TM}",
    )(cnt, off, pos, n2)


def _expert_kernel(te_ref, na_ref, x_ref, wgu_ref, wdn_ref, y_ref):
    used = pl.program_id(0) < na_ref[0]

    @pl.when(used)
    def _():
        gu = _dot(x_ref[...].astype(BF16), wgu_ref[...])
        g = gu[:, :D_FF_EXPERT]
        u = gu[:, D_FF_EXPERT:]
        y_ref[...] = _dot((g * _sigmoid(g) * u).astype(BF16), wdn_ref[...])

    @pl.when(jnp.logical_not(used))
    def _():
        y_ref[...] = jnp.zeros_like(y_ref)


def _experts(layer, TE, te, na, xs, W):
    P, D = xs.shape
    j = layer // 2
    rows = pl.BlockSpec((TE, D), lambda t, te, na: (t, 0))
    return pl.pallas_call(
        _expert_kernel,
        grid_spec=pltpu.PrefetchScalarGridSpec(
            num_scalar_prefetch=2, grid=(P // TE,),
            in_specs=[rows,
                      pl.BlockSpec((None, None, D, 2 * D_FF_EXPERT), lambda t, te, na: (j, te[t], 0, 0)),
                      pl.BlockSpec((None, None, D_FF_EXPERT, D), lambda t, te, na: (j, te[t], 0, 0))],
            out_specs=rows),
        out_shape=jax.ShapeDtypeStruct((P, D), F32),
        compiler_params=pltpu.CompilerParams(dimension_semantics=("arbitrary",),
                                             vmem_limit_bytes=VMEM_LIMIT),
        name=f"moe_experts_{TE}",
    )(te, na, xs, W["moe_w_gu"], W["moe_w_dn"])


def _combine_kernel(TM, pos_ref, h_ref, wts_ref, ys_hbm, gple_ref, wpg_ref, p_ref, wpp_ref,
                    hout_ref, y_buf, sem):
    def issue(r, c):
        _row_copy(ys_hbm, pos_ref[0, 0, 2 * r], y_buf.at[0], r, sem).start()
        _row_copy(ys_hbm, pos_ref[0, 0, 2 * r + 1], y_buf.at[1], r, sem).start(priority=1)
        return c

    def drain(r, c):
        _row_copy(ys_hbm, 0, y_buf.at[0], 0, sem).wait()
        _row_copy(ys_hbm, 0, y_buf.at[1], 0, sem).wait()
        return c

    lax.fori_loop(0, TM, issue, 0, unroll=8)
    lax.fori_loop(0, TM, drain, 0, unroll=8)
    moe = wts_ref[:, 0:1] * y_buf[0] + wts_ref[:, 1:2] * y_buf[1]
    hout_ref[...] = _ple(h_ref[...] + moe, gple_ref, wpg_ref, p_ref, wpp_ref)


def _combine(layer, TM, pos, h, wts, ys, p, W):
    N, D = h.shape
    row = pl.BlockSpec((TM, D), lambda i: (i, 0))
    in_specs = [pl.BlockSpec((1, 1, 2 * TM), lambda i: (i, 0, 0), memory_space=pltpu.SMEM),
                row, pl.BlockSpec((TM, LANES), lambda i: (i, 0)),
                pl.BlockSpec(memory_space=pl.ANY)] + _ple_specs(layer, TM, 1)
    return pl.pallas_call(
        functools.partial(_combine_kernel, TM),
        grid=(N // TM,),
        in_specs=in_specs, out_specs=row,
        out_shape=jax.ShapeDtypeStruct((N, D), F32),
        scratch_shapes=[pltpu.VMEM((2, TM, D), F32), pltpu.SemaphoreType.DMA],
        compiler_params=pltpu.CompilerParams(dimension_semantics=("arbitrary",),
                                             vmem_limit_bytes=VMEM_LIMIT, disable_bounds_checks=True),
        name=f"moe_combine_{TM}",
    )(pos, h, wts, ys, *_ple_args(p, W))


def _ffn_moe(layer, TM, TE, h, p, W):
    N = h.shape[0]
    n_tiles = 2 * N // TE + N_EXPERTS
    n2, idx, wts, cnt = _route(layer, TM, h, W)
    cnt = cnt[0, :N_EXPERTS]
    ends = jnp.cumsum((cnt + TE - 1) // TE).astype(jnp.int32)
    off = jnp.concatenate([jnp.zeros((1,), jnp.int32), ends * TE])
    experts = jnp.arange(N_EXPERTS, dtype=jnp.int32)

    def first_row(e):
        return jnp.sum(jnp.where(e[:, None] == experts[None, :], off[None, :N_EXPERTS], 0), axis=1)

    pos = jnp.stack([first_row(idx[:, 0]) + idx[:, 2], first_row(idx[:, 1]) + idx[:, 3]], axis=1)
    pos = pos.reshape(N // TM, 1, 2 * TM)
    t = jnp.minimum(jnp.arange(n_tiles, dtype=jnp.int32), ends[-1] - 1)
    te = jnp.sum(t[:, None] >= ends[None, :], axis=1).astype(jnp.int32)
    xs = _dispatch(TM, TE, n_tiles, cnt, off, pos, n2)
    ys = _experts(layer, TE, te, ends[-1:], xs, W)
    return _combine(layer, TM, pos, h, wts, ys, p, W)


def _norm_kernel(h_ref, g_ref, o_ref):
    o_ref[...] = _rms(h_ref[...], g_ref[...])


def _final_norm(TM, h, g):
    N = h.shape[0]
    row = pl.BlockSpec((TM, D_MODEL), lambda i: (i, 0))
    return pl.pallas_call(
        _norm_kernel, grid=(N // TM,),
        in_specs=[row, _const_spec((1, D_MODEL))], out_specs=row,
        out_shape=jax.ShapeDtypeStruct((N, D_MODEL), F32),
        compiler_params=pltpu.CompilerParams(dimension_semantics=("parallel",)),
        name=f"final_norm_{TM}",
    )(h, g)


def _block_diag_gates(wx, wa):
    eye = jnp.eye(HEADS_PER_CHUNK, dtype=wx.dtype)

    def bd(w):
        w = w.reshape(DEPTH, N_GATE_CHUNKS, HEADS_PER_CHUNK, LRU_BLOCK, LRU_BLOCK)
        w = jnp.einsum("dchij,hk->dchikj", w, eye)
        return w.reshape(DEPTH, N_GATE_CHUNKS, GATE_CHUNK, GATE_CHUNK)

    return jnp.concatenate([bd(wx), bd(wa)], axis=-1).astype(BF16)


def _run(h, p, G, TT, ffn_tm, moe_tm, moe_te, lru0, ca0, cb0, W):
    lru_s, ca_s, cb_s = [], [], []
    for layer in range(DEPTH):
        h, lru, ca, cb = _mixer(layer, G, TT, h, ca0[layer], cb0[layer], lru0[layer], W)
        if layer % 2 == 0:
            h = _ffn_dense(layer, ffn_tm, h, p, W)
        else:
            h = _ffn_moe(layer, moe_tm, moe_te, h, p, W)
        lru_s.append(lru)
        ca_s.append(ca)
        cb_s.append(cb)
    y = _final_norm(ffn_tm, h, W["final_norm"])
    return y, jnp.stack(lru_s), jnp.stack(ca_s), jnp.stack(cb_s)


def kernel(x_prompt, x_sample, state_lru_h, state_conv_a, state_conv_b, p_prompt, p_sample,
           norm_mix, w_in, conv_a_w, conv_a_b, lru_wx, lru_bx, lru_wa, lru_ba, lru_lambda,
           w_a_out, conv_b_w, w_b_out, w_o, norm_ffn, ffn_w_gu, ffn_w_dn, router,
           moe_w_gu, moe_w_dn, norm_ple, w_ple_gate, w_ple_proj, final_norm):
    B, T, D = x_prompt.shape
    S = x_sample.shape[0]
    KA, KB = CONV_A_WIDTH - 1, CONV_B_WIDTH - 1

    def vec(v):
        return v.reshape(v.shape[0], 1, D)

    W = {
        "norm_mix": vec(norm_mix), "w_in": w_in.astype(BF16),
        "conv_a_w": conv_a_w, "conv_a_b": vec(conv_a_b),
        "w_gate": _block_diag_gates(lru_wx, lru_wa),
        "lru_bx": vec(lru_bx), "lru_ba": vec(lru_ba), "lru_lambda": vec(lru_lambda),
        "w_a_out": w_a_out.astype(BF16), "conv_b_w": conv_b_w, "w_b_out": w_b_out.astype(BF16),
        "w_o": w_o.astype(BF16), "norm_ffn": vec(norm_ffn),
        "ffn_w_gu": ffn_w_gu.astype(BF16), "ffn_w_dn": ffn_w_dn.astype(BF16),
        "router": jnp.pad(router, ((0, 0), (0, 0), (0, LANES - N_EXPERTS))).astype(BF16),
        "moe_w_gu": moe_w_gu.astype(BF16), "moe_w_dn": moe_w_dn.astype(BF16),
        "norm_ple": vec(norm_ple), "w_ple_gate": w_ple_gate.astype(BF16),
        "w_ple_proj": w_ple_proj.astype(BF16), "final_norm": final_norm.reshape(1, D),
    }

    hp = x_prompt.transpose(1, 0, 2).reshape(T * B, D)
    pp = p_prompt.transpose(0, 2, 1, 3).reshape(DEPTH, T * B, PLE_DIM).astype(BF16)
    zeros = lambda k: jnp.zeros((DEPTH, k * B, D), F32)
    y_p, lru_p, ca_p, cb_p = _run(hp, pp, B, 64, 1024, 512, 512, zeros(1), zeros(KA), zeros(KB), W)
    y_p = y_p.reshape(T, B, D).transpose(1, 0, 2)
    ca_p = ca_p.reshape(DEPTH, KA, B, D).transpose(0, 2, 1, 3)
    cb_p = cb_p.reshape(DEPTH, KB, B, D).transpose(0, 2, 1, 3)

    hs = x_sample.reshape(S, D)
    ps = p_sample.reshape(DEPTH, S, PLE_DIM).astype(BF16)
    ca0 = state_conv_a.transpose(0, 2, 1, 3).reshape(DEPTH, KA * S, D)
    cb0 = state_conv_b.transpose(0, 2, 1, 3).reshape(DEPTH, KB * S, D)
    y_s, lru_d, ca_d, cb_d = _run(hs, ps, S, 1, S, S, S, state_lru_h, ca0, cb0, W)
    y_s = y_s.reshape(S, 1, D)
    ca_d = ca_d.reshape(DEPTH, KA, S, D).transpose(0, 2, 1, 3)
    cb_d = cb_d.reshape(DEPTH, KB, S, D).transpose(0, 2, 1, 3)

    return (y_p, y_s, lru_p, ca_p, cb_p, lru_d, ca_d, cb_d)
```

```python
import functools

import jax
import jax.numpy as jnp
from jax import lax
from jax.experimental import pallas as pl
from jax.experimental.pallas import tpu as pltpu

D_MODEL = 1024
DEPTH = 4
LRU_HEADS = 16
LRU_BLOCK = D_MODEL // LRU_HEADS
CONV_A_WIDTH = 4
CONV_B_WIDTH = 3
LRU_C = 8.0
PLE_DIM = 256
D_FF = 3 * D_MODEL
N_EXPERTS = 8
D_FF_EXPERT = 3 * D_MODEL // 2
EPS = 1e-6
IN_COLS = 7 * D_MODEL

LANES = 128
GATE_CHUNK = 256
N_GATE_CHUNKS = D_MODEL // GATE_CHUNK
HEADS_PER_CHUNK = GATE_CHUNK // LRU_BLOCK
FF_TILE = 1024
VMEM_LIMIT = 56 * 1024 * 1024

F32 = jnp.float32
BF16 = jnp.bfloat16


def _rms(x, g):
    var = jnp.mean(x * x, axis=-1, keepdims=True)
    return x * lax.rsqrt(var + EPS) * g


def _sigmoid(x):
    return 1.0 / (1.0 + jnp.exp(-x))


def _gelu_tanh(x):
    c = 0.7978845608028654
    return 0.5 * x * (1.0 + jnp.tanh(c * (x + 0.044715 * (x * x * x))))


def _dot(a, b):
    return jnp.dot(a, b, preferred_element_type=F32)


def _mixer_kernel(G, TT,
                  h_ref, ca0_ref, cb0_ref, l0_ref, gmix_ref, win_ref, caw_ref, cab_ref,
                  wg_ref, bx_ref, ba_ref, lam_ref, wao_ref, cbw_ref, wbo_ref, wo_ref,
                  hout_ref, lru_ref, ca_ref, cb_ref,
                  xa_buf, ub_buf, a_buf, b_buf, hstate):
    TM = TT * G
    HA = (CONV_A_WIDTH - 1) * G
    HB = (CONV_B_WIDTH - 1) * G
    D = D_MODEL

    @pl.when(pl.program_id(0) == 0)
    def _():
        xa_buf[0:HA, :] = ca0_ref[...]
        ub_buf[0:HB, :] = cb0_ref[...]
        hstate[...] = l0_ref[...]

    h = h_ref[...]
    n = _rms(h, gmix_ref[...]).astype(BF16)

    def proj(c):
        return _dot(n, win_ref[:, c * D:(c + 1) * D])

    xa = proj(0)
    xa_buf[HA:HA + TM, :] = xa
    ua = caw_ref[0:1, :] * xa_buf[0:TM, :]
    ua = ua + caw_ref[1:2, :] * xa_buf[G:G + TM, :]
    ua = ua + caw_ref[2:3, :] * xa_buf[2 * G:2 * G + TM, :]
    ua = ua + caw_ref[3:4, :] * xa
    ua = ua + cab_ref[...]
    ua_bf = ua.astype(BF16)

    lam = lam_ref[...]
    log_sig = -(jnp.maximum(-lam, 0.0) + jnp.log1p(jnp.exp(-jnp.abs(lam))))
    for c in range(N_GATE_CHUNKS):
        sl = slice(c * GATE_CHUNK, (c + 1) * GATE_CHUNK)
        g = _dot(ua_bf[:, sl], wg_ref[c])
        gate_x = _sigmoid(g[:, :GATE_CHUNK] + bx_ref[:, sl])
        gate_a = _sigmoid(g[:, GATE_CHUNK:] + ba_ref[:, sl])
        log_a = LRU_C * gate_a * log_sig[:, sl]
        a = jnp.exp(log_a)
        a_buf[:, sl] = a
        b_buf[:, sl] = jnp.sqrt(1.0 - a * a) * gate_x * ua[:, sl]

    def step(t, hc):
        r = pl.multiple_of(t * G, G)
        hn = a_buf[pl.ds(r, G), :] * hc + b_buf[pl.ds(r, G), :]
        b_buf[pl.ds(r, G), :] = hn
        return hn

    h_last = lax.fori_loop(0, TT, step, hstate[...], unroll=min(TT, 8))
    hstate[...] = h_last

    gated_a = (b_buf[...] * _gelu_tanh(proj(1))).astype(BF16)
    out_a = _dot(gated_a, wao_ref[...])

    ub_buf[HB:HB + TM, :] = proj(4) * proj(2)
    ub = cbw_ref[0:1, :] * ub_buf[0:TM, :]
    ub = ub + cbw_ref[1:2, :] * ub_buf[G:G + TM, :]
    ub = ub + cbw_ref[2:3, :] * ub_buf[2 * G:2 * G + TM, :]
    gated_b = (proj(3) * ub).astype(BF16)
    out_b = _dot(gated_b, wbo_ref[...])

    merged = _sigmoid(proj(5)) * out_a + _sigmoid(proj(6)) * out_b
    hout_ref[...] = h + _dot(merged.astype(BF16), wo_ref[...])

    new_ca = xa_buf[TM:TM + HA, :]
    new_cb = ub_buf[TM:TM + HB, :]
    xa_buf[0:HA, :] = new_ca
    ub_buf[0:HB, :] = new_cb
    ca_ref[...] = new_ca
    cb_ref[...] = new_cb
    lru_ref[...] = h_last


def _const_spec(shape, layer=None):
    if layer is None:
        return pl.BlockSpec(shape, lambda *_: (0,) * len(shape), pipeline_mode=pl.Buffered(1))
    return pl.BlockSpec((None,) + shape, lambda *_: (layer,) + (0,) * len(shape),
                        pipeline_mode=pl.Buffered(1))


def _mixer(layer, G, TT, h, ca0, cb0, l0, W):
    N = h.shape[0]
    TM = TT * G
    D = D_MODEL
    HA = (CONV_A_WIDTH - 1) * G
    HB = (CONV_B_WIDTH - 1) * G
    row = pl.BlockSpec((TM, D), lambda i: (i, 0))
    vec = _const_spec((1, D), layer)
    in_specs = [
        row,
        _const_spec((HA, D)), _const_spec((HB, D)), _const_spec((G, D)),
        vec,
        _const_spec((D, IN_COLS), layer),
        _const_spec((CONV_A_WIDTH, D), layer), vec,
        _const_spec((N_GATE_CHUNKS, GATE_CHUNK, 2 * GATE_CHUNK), layer),
        vec, vec, vec,
        _const_spec((D, D), layer),
        _const_spec((CONV_B_WIDTH, D), layer),
        _const_spec((D, D), layer),
        _const_spec((D, D), layer),
    ]
    out_specs = [row, _const_spec((G, D)), _const_spec((HA, D)), _const_spec((HB, D))]
    out_shape = [jax.ShapeDtypeStruct((N, D), F32), jax.ShapeDtypeStruct((G, D), F32),
                 jax.ShapeDtypeStruct((HA, D), F32), jax.ShapeDtypeStruct((HB, D), F32)]
    scratch = [pltpu.VMEM((HA + TM, D), F32), pltpu.VMEM((HB + TM, D), F32),
               pltpu.VMEM((TM, D), F32), pltpu.VMEM((TM, D), F32), pltpu.VMEM((G, D), F32)]
    return pl.pallas_call(
        functools.partial(_mixer_kernel, G, TT),
        grid=(N // TM,),
        in_specs=in_specs, out_specs=out_specs, out_shape=out_shape,
        scratch_shapes=scratch,
        compiler_params=pltpu.CompilerParams(dimension_semantics=("arbitrary",),
                                             vmem_limit_bytes=VMEM_LIMIT),
        name=f"mixer_g{G}",
    )(h, ca0, cb0, l0, W["norm_mix"], W["w_in"], W["conv_a_w"], W["conv_a_b"], W["w_gate"],
      W["lru_bx"], W["lru_ba"], W["lru_lambda"], W["w_a_out"], W["conv_b_w"], W["w_b_out"], W["w_o"])


def _ple(h1, gple_ref, wpg_ref, p_ref, wpp_ref):
    n3 = _rms(h1, gple_ref[...]).astype(BF16)
    gate = _sigmoid(_dot(n3, wpg_ref[...]))
    return h1 + gate * _dot(p_ref[...], wpp_ref[...])


def _ple_specs(layer, TM, ngrid):
    if ngrid == 1:
        p_spec = pl.BlockSpec((None, TM, PLE_DIM), lambda i: (layer, i, 0))
    else:
        p_spec = pl.BlockSpec((None, TM, PLE_DIM), lambda i, k: (layer, i, 0))
    return [_const_spec((1, D_MODEL), layer), _const_spec((D_MODEL, D_MODEL), layer), p_spec,
            _const_spec((PLE_DIM, D_MODEL), layer)]


def _ple_args(p, W):
    return [W["norm_ple"], W["w_ple_gate"], p, W["w_ple_proj"]]


def _ffn_kernel(nk, h_ref, gffn_ref, wg_ref, wu_ref, wd_ref, gple_ref, wpg_ref, p_ref, wpp_ref,
                hout_ref, n2_buf, acc):
    k = pl.program_id(1)

    @pl.when(k == 0)
    def _():
        n2_buf[...] = _rms(h_ref[...], gffn_ref[...]).astype(BF16)
        acc[...] = jnp.zeros_like(acc)

    x = n2_buf[...]
    g = _dot(x, wg_ref[...])
    u = _dot(x, wu_ref[...])
    acc[...] += _dot((g * _sigmoid(g) * u).astype(BF16), wd_ref[...])

    @pl.when(k == nk - 1)
    def _():
        hout_ref[...] = _ple(h_ref[...] + acc[...], gple_ref, wpg_ref, p_ref, wpp_ref)


def _ffn_dense(layer, TM, h, p, W):
    N = h.shape[0]
    D = D_MODEL
    j = layer // 2
    nk = D_FF // FF_TILE
    row = pl.BlockSpec((TM, D), lambda i, k: (i, 0))
    in_specs = [row, _const_spec((1, D), layer),
                pl.BlockSpec((None, D, FF_TILE), lambda i, k: (j, 0, k)),
                pl.BlockSpec((None, D, FF_TILE), lambda i, k: (j, 0, nk + k)),
                pl.BlockSpec((None, FF_TILE, D), lambda i, k: (j, k, 0))] + _ple_specs(layer, TM, 2)
    return pl.pallas_call(
        functools.partial(_ffn_kernel, nk),
        grid=(N // TM, nk),
        in_specs=in_specs, out_specs=row,
        out_shape=jax.ShapeDtypeStruct((N, D), F32),
        scratch_shapes=[pltpu.VMEM((TM, D), BF16), pltpu.VMEM((TM, D), F32)],
        compiler_params=pltpu.CompilerParams(dimension_semantics=("parallel", "arbitrary"),
                                             vmem_limit_bytes=VMEM_LIMIT),
        name=f"ffn_dense_{TM}",
    )(h, W["norm_ffn"], W["ffn_w_gu"], W["ffn_w_gu"], W["ffn_w_dn"], *_ple_args(p, W))


def _route_kernel(h_ref, gffn_ref, router_ref, n2_ref, idx_ref, wts_ref, cnt_ref, carry):
    TM = h_ref.shape[0]

    @pl.when(pl.program_id(0) == 0)
    def _():
        carry[...] = jnp.zeros_like(carry)

    n2 = _rms(h_ref[...], gffn_ref[...])
    n2_ref[...] = n2
    lane = lax.broadcasted_iota(jnp.int32, (TM, LANES), 1)
    logits = jnp.dot(n2, router_ref[...], preferred_element_type=F32, precision=lax.Precision.HIGHEST)
    logits = jnp.where(lane < N_EXPERTS, logits, -jnp.inf)
    v1 = jnp.max(logits, axis=-1, keepdims=True)
    i1 = jnp.min(jnp.where(logits == v1, lane, LANES), axis=-1, keepdims=True)
    rest = jnp.where(lane == i1, -jnp.inf, logits)
    v2 = jnp.max(rest, axis=-1, keepdims=True)
    i2 = jnp.min(jnp.where(rest == v2, lane, LANES), axis=-1, keepdims=True)
    ex = jnp.exp(v2 - v1)
    w1 = 1.0 / (1.0 + ex)
    w2 = ex / (1.0 + ex)
    m1 = lane == i1
    m2 = lane == i2
    chosen = jnp.where(m1 | m2, 1.0, 0.0)
    before = (lax.broadcasted_iota(jnp.int32, (TM, TM), 1)
              < lax.broadcasted_iota(jnp.int32, (TM, TM), 0))
    counts = _dot(jnp.where(before, 1.0, 0.0).astype(BF16), chosen.astype(BF16)) + carry[...]
    r1 = jnp.sum(jnp.where(m1, counts, 0.0), axis=-1, keepdims=True).astype(jnp.int32)
    r2 = jnp.sum(jnp.where(m2, counts, 0.0), axis=-1, keepdims=True).astype(jnp.int32)
    carry[...] += jnp.sum(chosen, axis=0, keepdims=True)
    cnt_ref[...] = carry[...].astype(jnp.int32)
    idx_ref[...] = jnp.where(lane == 0, i1, jnp.where(lane == 1, i2,
                             jnp.where(lane == 2, r1, jnp.where(lane == 3, r2, 0))))
    wts_ref[...] = jnp.where(lane == 0, w1, jnp.where(lane == 1, w2, 0.0))


def _route(layer, TM, h, W):
    N = h.shape[0]
    D = D_MODEL
    row = pl.BlockSpec((TM, D), lambda i: (i, 0))
    info = pl.BlockSpec((TM, LANES), lambda i: (i, 0))
    return pl.pallas_call(
        _route_kernel, grid=(N // TM,),
        in_specs=[row, _const_spec((1, D), layer), _const_spec((D, LANES), layer // 2)],
        out_specs=[row, info, info, _const_spec((1, LANES))],
        out_shape=[jax.ShapeDtypeStruct((N, D), F32), jax.ShapeDtypeStruct((N, LANES), jnp.int32),
                   jax.ShapeDtypeStruct((N, LANES), F32), jax.ShapeDtypeStruct((1, LANES), jnp.int32)],
        scratch_shapes=[pltpu.VMEM((1, LANES), F32)],
        compiler_params=pltpu.CompilerParams(dimension_semantics=("arbitrary",)),
        name=f"moe_route_{TM}",
    )(h, W["norm_ffn"], W["router"])


def _row_copy(src, s, dst, d, sem):
    return pltpu.make_async_copy(src.at[pl.ds(s, 1)], dst.at[pl.ds(d, 1)], sem)


def _dispatch_kernel(TM, TE, n_tiles, cnt_ref, off_ref, pos_ref, n2_ref, xs_hbm, zero_buf, sem, pad_sem):
    i = pl.program_id(0)

    @pl.when(i == 0)
    def _():
        zero_buf[...] = jnp.zeros_like(zero_buf)
        for e in range(N_EXPERTS):
            lo = off_ref[e] + cnt_ref[e]
            hi = off_ref[e + 1]

            def pad(r, c):
                _row_copy(zero_buf, 0, xs_hbm, r, pad_sem).start()
                return c

            def pad_wait(r, c):
                _row_copy(zero_buf, 0, xs_hbm, 0, pad_sem).wait()
                return c

            lax.fori_loop(lo, hi, pad, 0)
            lax.fori_loop(lo, hi, pad_wait, 0)

        def tile_copy(t):
            return pltpu.make_async_copy(zero_buf, xs_hbm.at[pl.ds(pl.multiple_of(t * TE, TE), TE)], pad_sem)

        def fill(t, c):
            tile_copy(t).start()
            return c

        def fill_wait(t, c):
            tile_copy(t).wait()
            return c

        first_unused = off_ref[N_EXPERTS] // TE
        lax.fori_loop(first_unused, n_tiles, fill, 0)
        lax.fori_loop(first_unused, n_tiles, fill_wait, 0)

    def issue(r, c):
        _row_copy(n2_ref, r, xs_hbm, pos_ref[0, 0, 2 * r], sem).start()
        _row_copy(n2_ref, r, xs_hbm, pos_ref[0, 0, 2 * r + 1], sem).start(priority=1)
        return c

    def drain(r, c):
        _row_copy(n2_ref, 0, xs_hbm, 0, sem).wait()
        _row_copy(n2_ref, 0, xs_hbm, 0, sem).wait()
        return c

    lax.fori_loop(0, TM, issue, 0, unroll=8)
    lax.fori_loop(0, TM, drain, 0, unroll=8)


def _dispatch(TM, TE, n_tiles, cnt, off, pos, n2):
    N, D = n2.shape
    return pl.pallas_call(
        functools.partial(_dispatch_kernel, TM, TE, n_tiles),
        grid_spec=pltpu.PrefetchScalarGridSpec(
            num_scalar_prefetch=2, grid=(N // TM,),
            in_specs=[pl.BlockSpec((1, 1, 2 * TM), lambda i, c, o: (i, 0, 0), memory_space=pltpu.SMEM),
                      pl.BlockSpec((TM, D), lambda i, c, o: (i, 0))],
            out_specs=pl.BlockSpec(memory_space=pl.ANY),
            scratch_shapes=[pltpu.VMEM((TE, D), F32), pltpu.SemaphoreType.DMA, pltpu.SemaphoreType.DMA]),
        out_shape=jax.ShapeDtypeStruct((n_tiles * TE, D), F32),
        compiler_params=pltpu.CompilerParams(dimension_semantics=("arbitrary",),
                                             disable_bounds_checks=True),
        name=f"moe_dispatch_{TM}",
    )(cnt, off, pos, n2)


def _expert_kernel(te_ref, na_ref, x_ref, wgu_ref, wdn_ref, y_ref):
    used = pl.program_id(0) < na_ref[0]

    @pl.when(used)
    def _():
        gu = _dot(x_ref[...].astype(BF16), wgu_ref[...])
        g = gu[:, :D_FF_EXPERT]
        u = gu[:, D_FF_EXPERT:]
        y_ref[...] = _dot((g * _sigmoid(g) * u).astype(BF16), wdn_ref[...])

    @pl.when(jnp.logical_not(used))
    def _():
        y_ref[...] = jnp.zeros_like(y_ref)


def _experts(layer, TE, te, na, xs, W):
    P, D = xs.shape
    j = layer // 2
    rows = pl.BlockSpec((TE, D), lambda t, te, na: (t, 0))
    return pl.pallas_call(
        _expert_kernel,
        grid_spec=pltpu.PrefetchScalarGridSpec(
            num_scalar_prefetch=2, grid=(P // TE,),
            in_specs=[rows,
                      pl.BlockSpec((None, None, D, 2 * D_FF_EXPERT), lambda t, te, na: (j, te[t], 0, 0)),
                      pl.BlockSpec((None, None, D_FF_EXPERT, D), lambda t, te, na: (j, te[t], 0, 0))],
            out_specs=rows),
        out_shape=jax.ShapeDtypeStruct((P, D), F32),
        compiler_params=pltpu.CompilerParams(dimension_semantics=("arbitrary",),
                                             vmem_limit_bytes=VMEM_LIMIT),
        name=f"moe_experts_{TE}",
    )(te, na, xs, W["moe_w_gu"], W["moe_w_dn"])


def _combine_kernel(TM, pos_ref, h_ref, wts_ref, ys_hbm, gple_ref, wpg_ref, p_ref, wpp_ref,
                    hout_ref, y_buf, sem):
    def issue(r, c):
        _row_copy(ys_hbm, pos_ref[0, 0, 2 * r], y_buf.at[0], r, sem).start()
        _row_copy(ys_hbm, pos_ref[0, 0, 2 * r + 1], y_buf.at[1], r, sem).start(priority=1)
        return c

    def drain(r, c):
        _row_copy(ys_hbm, 0, y_buf.at[0], 0, sem).wait()
        _row_copy(ys_hbm, 0, y_buf.at[1], 0, sem).wait()
        return c

    lax.fori_loop(0, TM, issue, 0, unroll=8)
    lax.fori_loop(0, TM, drain, 0, unroll=8)
    moe = wts_ref[:, 0:1] * y_buf[0] + wts_ref[:, 1:2] * y_buf[1]
    hout_ref[...] = _ple(h_ref[...] + moe, gple_ref, wpg_ref, p_ref, wpp_ref)


def _combine(layer, TM, pos, h, wts, ys, p, W):
    N, D = h.shape
    row = pl.BlockSpec((TM, D), lambda i: (i, 0))
    in_specs = [pl.BlockSpec((1, 1, 2 * TM), lambda i: (i, 0, 0), memory_space=pltpu.SMEM),
                row, pl.BlockSpec((TM, LANES), lambda i: (i, 0)),
                pl.BlockSpec(memory_space=pl.ANY)] + _ple_specs(layer, TM, 1)
    return pl.pallas_call(
        functools.partial(_combine_kernel, TM),
        grid=(N // TM,),
        in_specs=in_specs, out_specs=row,
        out_shape=jax.ShapeDtypeStruct((N, D), F32),
        scratch_shapes=[pltpu.VMEM((2, TM, D), F32), pltpu.SemaphoreType.DMA],
        compiler_params=pltpu.CompilerParams(dimension_semantics=("arbitrary",),
                                             vmem_limit_bytes=VMEM_LIMIT, disable_bounds_checks=True),
        name=f"moe_combine_{TM}",
    )(pos, h, wts, ys, *_ple_args(p, W))


def _ffn_moe(layer, TM, TE, h, p, W):
    N = h.shape[0]
    n_tiles = 2 * N // TE + N_EXPERTS
    n2, idx, wts, cnt = _route(layer, TM, h, W)
    cnt = cnt[0, :N_EXPERTS]
    ends = jnp.cumsum((cnt + TE - 1) // TE).astype(jnp.int32)
    off = jnp.concatenate([jnp.zeros((1,), jnp.int32), ends * TE])
    experts = jnp.arange(N_EXPERTS, dtype=jnp.int32)

    def first_row(e):
        return jnp.sum(jnp.where(e[:, None] == experts[None, :], off[None, :N_EXPERTS], 0), axis=1)

    pos = jnp.stack([first_row(idx[:, 0]) + idx[:, 2], first_row(idx[:, 1]) + idx[:, 3]], axis=1)
    pos = pos.reshape(N // TM, 1, 2 * TM)
    t = jnp.minimum(jnp.arange(n_tiles, dtype=jnp.int32), ends[-1] - 1)
    te = jnp.sum(t[:, None] >= ends[None, :], axis=1).astype(jnp.int32)
    xs = _dispatch(TM, TE, n_tiles, cnt, off, pos, n2)
    ys = _experts(layer, TE, te, ends[-1:], xs, W)
    return _combine(layer, TM, pos, h, wts, ys, p, W)


def _norm_kernel(h_ref, g_ref, o_ref):
    o_ref[...] = _rms(h_ref[...], g_ref[...])


def _final_norm(TM, G, h, g):
    N = h.shape[0]
    row = pl.BlockSpec((TM, D_MODEL), lambda i: (i, 0))
    y = pl.pallas_call(
        _norm_kernel, grid=(N // TM,),
        in_specs=[row, _const_spec((1, D_MODEL))], out_specs=row,
        out_shape=jax.ShapeDtypeStruct((N, D_MODEL), F32),
        compiler_params=pltpu.CompilerParams(dimension_semantics=("parallel",)),
        name=f"final_norm_{TM}",
    )(h, g)
    return y.reshape(N // G, G, D_MODEL).transpose(1, 0, 2)


def _block_diag_gates(wx, wa):
    eye = jnp.eye(HEADS_PER_CHUNK, dtype=wx.dtype)

    def bd(w):
        w = w.reshape(DEPTH, N_GATE_CHUNKS, HEADS_PER_CHUNK, LRU_BLOCK, LRU_BLOCK)
        w = jnp.einsum("dchij,hk->dchikj", w, eye)
        return w.reshape(DEPTH, N_GATE_CHUNKS, GATE_CHUNK, GATE_CHUNK)

    return jnp.concatenate([bd(wx), bd(wa)], axis=-1).astype(BF16)


def _run(h, p, G, TT, ffn_tm, moe_tm, moe_te, lru0, ca0, cb0, W):
    lru_s, ca_s, cb_s = [], [], []
    for layer in range(DEPTH):
        h, lru, ca, cb = _mixer(layer, G, TT, h, ca0[layer], cb0[layer], lru0[layer], W)
        if layer % 2 == 0:
            h = _ffn_dense(layer, ffn_tm, h, p, W)
        else:
            h = _ffn_moe(layer, moe_tm, moe_te, h, p, W)
        lru_s.append(lru)
        ca_s.append(ca)
        cb_s.append(cb)
    y = _final_norm(ffn_tm, G, h, W["final_norm"])
    return y, jnp.stack(lru_s), jnp.stack(ca_s), jnp.stack(cb_s)


def kernel(x_prompt, x_sample, state_lru_h, state_conv_a, state_conv_b, p_prompt, p_sample,
           norm_mix, w_in, conv_a_w, conv_a_b, lru_wx, lru_bx, lru_wa, lru_ba, lru_lambda,
           w_a_out, conv_b_w, w_b_out, w_o, norm_ffn, ffn_w_gu, ffn_w_dn, router,
           moe_w_gu, moe_w_dn, norm_ple, w_ple_gate, w_ple_proj, final_norm):
    B, T, D = x_prompt.shape
    S = x_sample.shape[0]
    KA, KB = CONV_A_WIDTH - 1, CONV_B_WIDTH - 1

    def vec(v):
        return v.reshape(v.shape[0], 1, D)

    W = {
        "norm_mix": vec(norm_mix), "w_in": w_in.astype(BF16),
        "conv_a_w": conv_a_w, "conv_a_b": vec(conv_a_b),
        "w_gate": _block_diag_gates(lru_wx, lru_wa),
        "lru_bx": vec(lru_bx), "lru_ba": vec(lru_ba), "lru_lambda": vec(lru_lambda),
        "w_a_out": w_a_out.astype(BF16), "conv_b_w": conv_b_w, "w_b_out": w_b_out.astype(BF16),
        "w_o": w_o.astype(BF16), "norm_ffn": vec(norm_ffn),
        "ffn_w_gu": ffn_w_gu.astype(BF16), "ffn_w_dn": ffn_w_dn.astype(BF16),
        "router": jnp.pad(router, ((0, 0), (0, 0), (0, LANES - N_EXPERTS))),
        "moe_w_gu": moe_w_gu.astype(BF16), "moe_w_dn": moe_w_dn.astype(BF16),
        "norm_ple": vec(norm_ple), "w_ple_gate": w_ple_gate.astype(BF16),
        "w_ple_proj": w_ple_proj.astype(BF16), "final_norm": final_norm.reshape(1, D),
    }

    hp = x_prompt.transpose(1, 0, 2).reshape(T * B, D)
    pp = p_prompt.transpose(0, 2, 1, 3).reshape(DEPTH, T * B, PLE_DIM).astype(BF16)
    zeros = lambda k: jnp.zeros((DEPTH, k * B, D), F32)
    y_p, lru_p, ca_p, cb_p = _run(hp, pp, B, 64, 1024, 512, 512, zeros(1), zeros(KA), zeros(KB), W)
    ca_p = ca_p.reshape(DEPTH, KA, B, D).transpose(0, 2, 1, 3)
    cb_p = cb_p.reshape(DEPTH, KB, B, D).transpose(0, 2, 1, 3)

    hs = x_sample.reshape(S, D)
    ps = p_sample.reshape(DEPTH, S, PLE_DIM).astype(BF16)
    ca0 = state_conv_a.transpose(0, 2, 1, 3).reshape(DEPTH, KA * S, D)
    cb0 = state_conv_b.transpose(0, 2, 1, 3).reshape(DEPTH, KB * S, D)
    y_s, lru_d, ca_d, cb_d = _run(hs, ps, S, 1, S, S, S, state_lru_h, ca0, cb0, W)
    ca_d = ca_d.reshape(DEPTH, KA, S, D).transpose(0, 2, 1, 3)
    cb_d = cb_d.reshape(DEPTH, KB, S, D).transpose(0, 2, 1, 3)

    return (y_p, y_s, lru_p, ca_p, cb_p, lru_d, ca_d, cb_d)
```

```python
import functools

import jax
import jax.numpy as jnp
from jax import lax
from jax.experimental import pallas as pl
from jax.experimental.pallas import tpu as pltpu

D_MODEL = 1024
DEPTH = 4
LRU_HEADS = 16
LRU_BLOCK = D_MODEL // LRU_HEADS
CONV_A_WIDTH = 4
CONV_B_WIDTH = 3
LRU_C = 8.0
PLE_DIM = 256
D_FF = 3 * D_MODEL
N_EXPERTS = 8
D_FF_EXPERT = 3 * D_MODEL // 2
EPS = 1e-6
IN_COLS = 7 * D_MODEL

LANES = 128
GATE_CHUNK = 256
N_GATE_CHUNKS = D_MODEL // GATE_CHUNK
HEADS_PER_CHUNK = GATE_CHUNK // LRU_BLOCK
FF_TILE = 1024
VMEM_LIMIT = 56 * 1024 * 1024

F32 = jnp.float32
BF16 = jnp.bfloat16


def _rms(x, g):
    var = jnp.mean(x * x, axis=-1, keepdims=True)
    return x * lax.rsqrt(var + EPS) * g


def _sigmoid(x):
    return 1.0 / (1.0 + jnp.exp(-x))


def _gelu_tanh(x):
    c = 0.7978845608028654
    return 0.5 * x * (1.0 + jnp.tanh(c * (x + 0.044715 * (x * x * x))))


def _dot(a, b):
    return jnp.dot(a, b, preferred_element_type=F32)


def _mixer_kernel(G, TT,
                  h_ref, ca0_ref, cb0_ref, l0_ref, gmix_ref, win_ref, caw_ref, cab_ref,
                  wg_ref, bx_ref, ba_ref, lam_ref, wao_ref, cbw_ref, wbo_ref, wo_ref,
                  hout_ref, lru_ref, ca_ref, cb_ref,
                  xa_buf, ub_buf, a_buf, b_buf, hstate):
    TM = TT * G
    HA = (CONV_A_WIDTH - 1) * G
    HB = (CONV_B_WIDTH - 1) * G
    D = D_MODEL

    @pl.when(pl.program_id(0) == 0)
    def _():
        xa_buf[0:HA, :] = ca0_ref[...]
        ub_buf[0:HB, :] = cb0_ref[...]
        hstate[...] = l0_ref[...]

    h = h_ref[...]
    n = _rms(h, gmix_ref[...]).astype(BF16)

    def proj(c):
        return _dot(n, win_ref[:, c * D:(c + 1) * D])

    xa = proj(0)
    xa_buf[HA:HA + TM, :] = xa
    ua = caw_ref[0:1, :] * xa_buf[0:TM, :]
    ua = ua + caw_ref[1:2, :] * xa_buf[G:G + TM, :]
    ua = ua + caw_ref[2:3, :] * xa_buf[2 * G:2 * G + TM, :]
    ua = ua + caw_ref[3:4, :] * xa
    ua = ua + cab_ref[...]
    ua_bf = ua.astype(BF16)

    lam = lam_ref[...]
    log_sig = -(jnp.maximum(-lam, 0.0) + jnp.log1p(jnp.exp(-jnp.abs(lam))))
    for c in range(N_GATE_CHUNKS):
        sl = slice(c * GATE_CHUNK, (c + 1) * GATE_CHUNK)
        g = _dot(ua_bf[:, sl], wg_ref[c])
        gate_x = _sigmoid(g[:, :GATE_CHUNK] + bx_ref[:, sl])
        gate_a = _sigmoid(g[:, GATE_CHUNK:] + ba_ref[:, sl])
        log_a = LRU_C * gate_a * log_sig[:, sl]
        a = jnp.exp(log_a)
        a_buf[:, sl] = a
        b_buf[:, sl] = jnp.sqrt(1.0 - a * a) * gate_x * ua[:, sl]

    def step(t, hc):
        r = pl.multiple_of(t * G, G)
        hn = a_buf[pl.ds(r, G), :] * hc + b_buf[pl.ds(r, G), :]
        b_buf[pl.ds(r, G), :] = hn
        return hn

    h_last = lax.fori_loop(0, TT, step, hstate[...], unroll=min(TT, 8))
    hstate[...] = h_last

    gated_a = (b_buf[...] * _gelu_tanh(proj(1))).astype(BF16)
    out_a = _dot(gated_a, wao_ref[...])

    ub_buf[HB:HB + TM, :] = proj(4) * proj(2)
    ub = cbw_ref[0:1, :] * ub_buf[0:TM, :]
    ub = ub + cbw_ref[1:2, :] * ub_buf[G:G + TM, :]
    ub = ub + cbw_ref[2:3, :] * ub_buf[2 * G:2 * G + TM, :]
    gated_b = (proj(3) * ub).astype(BF16)
    out_b = _dot(gated_b, wbo_ref[...])

    merged = _sigmoid(proj(5)) * out_a + _sigmoid(proj(6)) * out_b
    hout_ref[...] = h + _dot(merged.astype(BF16), wo_ref[...])

    new_ca = xa_buf[TM:TM + HA, :]
    new_cb = ub_buf[TM:TM + HB, :]
    xa_buf[0:HA, :] = new_ca
    ub_buf[0:HB, :] = new_cb
    ca_ref[...] = new_ca
    cb_ref[...] = new_cb
    lru_ref[...] = h_last


def _const_spec(shape, layer=None):
    if layer is None:
        return pl.BlockSpec(shape, lambda *_: (0,) * len(shape), pipeline_mode=pl.Buffered(1))
    return pl.BlockSpec((None,) + shape, lambda *_: (layer,) + (0,) * len(shape),
                        pipeline_mode=pl.Buffered(1))


def _mixer(layer, G, TT, h, ca0, cb0, l0, W):
    N = h.shape[0]
    TM = TT * G
    D = D_MODEL
    HA = (CONV_A_WIDTH - 1) * G
    HB = (CONV_B_WIDTH - 1) * G
    row = pl.BlockSpec((TM, D), lambda i: (i, 0))
    vec = _const_spec((1, D), layer)
    in_specs = [
        row,
        _const_spec((HA, D)), _const_spec((HB, D)), _const_spec((G, D)),
        vec,
        _const_spec((D, IN_COLS), layer),
        _const_spec((CONV_A_WIDTH, D), layer), vec,
        _const_spec((N_GATE_CHUNKS, GATE_CHUNK, 2 * GATE_CHUNK), layer),
        vec, vec, vec,
        _const_spec((D, D), layer),
        _const_spec((CONV_B_WIDTH, D), layer),
        _const_spec((D, D), layer),
        _const_spec((D, D), layer),
    ]
    out_specs = [row, _const_spec((G, D)), _const_spec((HA, D)), _const_spec((HB, D))]
    out_shape = [jax.ShapeDtypeStruct((N, D), F32), jax.ShapeDtypeStruct((G, D), F32),
                 jax.ShapeDtypeStruct((HA, D), F32), jax.ShapeDtypeStruct((HB, D), F32)]
    scratch = [pltpu.VMEM((HA + TM, D), F32), pltpu.VMEM((HB + TM, D), F32),
               pltpu.VMEM((TM, D), F32), pltpu.VMEM((TM, D), F32), pltpu.VMEM((G, D), F32)]
    return pl.pallas_call(
        functools.partial(_mixer_kernel, G, TT),
        grid=(N // TM,),
        in_specs=in_specs, out_specs=out_specs, out_shape=out_shape,
        scratch_shapes=scratch,
        compiler_params=pltpu.CompilerParams(dimension_semantics=("arbitrary",),
                                             vmem_limit_bytes=VMEM_LIMIT),
        name=f"mixer_g{G}",
    )(h, ca0, cb0, l0, W["norm_mix"], W["w_in"], W["conv_a_w"], W["conv_a_b"], W["w_gate"],
      W["lru_bx"], W["lru_ba"], W["lru_lambda"], W["w_a_out"], W["conv_b_w"], W["w_b_out"], W["w_o"])


def _ple(h1, gple_ref, wpg_ref, p_ref, wpp_ref):
    n3 = _rms(h1, gple_ref[...]).astype(BF16)
    gate = _sigmoid(_dot(n3, wpg_ref[...]))
    return h1 + gate * _dot(p_ref[...], wpp_ref[...])


def _ple_specs(layer, TM, ngrid):
    if ngrid == 1:
        p_spec = pl.BlockSpec((None, TM, PLE_DIM), lambda i: (layer, i, 0))
    else:
        p_spec = pl.BlockSpec((None, TM, PLE_DIM), lambda i, k: (layer, i, 0))
    return [_const_spec((1, D_MODEL), layer), _const_spec((D_MODEL, D_MODEL), layer), p_spec,
            _const_spec((PLE_DIM, D_MODEL), layer)]


def _ple_args(p, W):
    return [W["norm_ple"], W["w_ple_gate"], p, W["w_ple_proj"]]


def _ffn_kernel(nk, h_ref, gffn_ref, wg_ref, wu_ref, wd_ref, gple_ref, wpg_ref, p_ref, wpp_ref,
                hout_ref, n2_buf, acc):
    k = pl.program_id(1)

    @pl.when(k == 0)
    def _():
        n2_buf[...] = _rms(h_ref[...], gffn_ref[...]).astype(BF16)
        acc[...] = jnp.zeros_like(acc)

    x = n2_buf[...]
    g = _dot(x, wg_ref[...])
    u = _dot(x, wu_ref[...])
    acc[...] += _dot((g * _sigmoid(g) * u).astype(BF16), wd_ref[...])

    @pl.when(k == nk - 1)
    def _():
        hout_ref[...] = _ple(h_ref[...] + acc[...], gple_ref, wpg_ref, p_ref, wpp_ref)


def _ffn_dense(layer, TM, h, p, W):
    N = h.shape[0]
    D = D_MODEL
    j = layer // 2
    nk = D_FF // FF_TILE
    row = pl.BlockSpec((TM, D), lambda i, k: (i, 0))
    in_specs = [row, _const_spec((1, D), layer),
                pl.BlockSpec((None, D, FF_TILE), lambda i, k: (j, 0, k)),
                pl.BlockSpec((None, D, FF_TILE), lambda i, k: (j, 0, nk + k)),
                pl.BlockSpec((None, FF_TILE, D), lambda i, k: (j, k, 0))] + _ple_specs(layer, TM, 2)
    return pl.pallas_call(
        functools.partial(_ffn_kernel, nk),
        grid=(N // TM, nk),
        in_specs=in_specs, out_specs=row,
        out_shape=jax.ShapeDtypeStruct((N, D), F32),
        scratch_shapes=[pltpu.VMEM((TM, D), BF16), pltpu.VMEM((TM, D), F32)],
        compiler_params=pltpu.CompilerParams(dimension_semantics=("parallel", "arbitrary"),
                                             vmem_limit_bytes=VMEM_LIMIT),
        name=f"ffn_dense_{TM}",
    )(h, W["norm_ffn"], W["ffn_w_gu"], W["ffn_w_gu"], W["ffn_w_dn"], *_ple_args(p, W))


def _route_kernel(h_ref, gffn_ref, router_ref, n2_ref, idx_ref, wts_ref, cnt_ref, carry):
    TM = h_ref.shape[0]

    @pl.when(pl.program_id(0) == 0)
    def _():
        carry[...] = jnp.zeros_like(carry)

    n2 = _rms(h_ref[...], gffn_ref[...])
    n2_ref[...] = n2
    lane = lax.broadcasted_iota(jnp.int32, (TM, LANES), 1)
    logits = jnp.dot(n2, router_ref[...], preferred_element_type=F32, precision=lax.Precision.HIGHEST)
    logits = jnp.where(lane < N_EXPERTS, logits, -jnp.inf)
    v1 = jnp.max(logits, axis=-1, keepdims=True)
    i1 = jnp.min(jnp.where(logits == v1, lane, LANES), axis=-1, keepdims=True)
    rest = jnp.where(lane == i1, -jnp.inf, logits)
    v2 = jnp.max(rest, axis=-1, keepdims=True)
    i2 = jnp.min(jnp.where(rest == v2, lane, LANES), axis=-1, keepdims=True)
    ex = jnp.exp(v2 - v1)
    w1 = 1.0 / (1.0 + ex)
    w2 = ex / (1.0 + ex)
    m1 = lane == i1
    m2 = lane == i2
    chosen = jnp.where(m1 | m2, 1.0, 0.0)
    before = (lax.broadcasted_iota(jnp.int32, (TM, TM), 1)
              < lax.broadcasted_iota(jnp.int32, (TM, TM), 0))
    counts = _dot(jnp.where(before, 1.0, 0.0).astype(BF16), chosen.astype(BF16)) + carry[...]
    r1 = jnp.sum(jnp.where(m1, counts, 0.0), axis=-1, keepdims=True).astype(jnp.int32)
    r2 = jnp.sum(jnp.where(m2, counts, 0.0), axis=-1, keepdims=True).astype(jnp.int32)
    carry[...] += jnp.sum(chosen, axis=0, keepdims=True)
    cnt_ref[...] = carry[...].astype(jnp.int32)
    idx_ref[...] = jnp.where(lane == 0, i1, jnp.where(lane == 1, i2,
                             jnp.where(lane == 2, r1, jnp.where(lane == 3, r2, 0))))
    wts_ref[...] = jnp.where(lane == 0, w1, jnp.where(lane == 1, w2, 0.0))


def _route(layer, TM, h, W):
    N = h.shape[0]
    D = D_MODEL
    row = pl.BlockSpec((TM, D), lambda i: (i, 0))
    info = pl.BlockSpec((TM, LANES), lambda i: (i, 0))
    return pl.pallas_call(
        _route_kernel, grid=(N // TM,),
        in_specs=[row, _const_spec((1, D), layer), _const_spec((D, LANES), layer // 2)],
        out_specs=[row, info, info, _const_spec((1, LANES))],
        out_shape=[jax.ShapeDtypeStruct((N, D), F32), jax.ShapeDtypeStruct((N, LANES), jnp.int32),
                   jax.ShapeDtypeStruct((N, LANES), F32), jax.ShapeDtypeStruct((1, LANES), jnp.int32)],
        scratch_shapes=[pltpu.VMEM((1, LANES), F32)],
        compiler_params=pltpu.CompilerParams(dimension_semantics=("arbitrary",)),
        name=f"moe_route_{TM}",
    )(h, W["norm_ffn"], W["router"])


def _row_copy(src, s, dst, d, sem):
    return pltpu.make_async_copy(src.at[pl.ds(s, 1)], dst.at[pl.ds(d, 1)], sem)


def _dispatch_kernel(TM, TE, n_tiles, cnt_ref, off_ref, pos_ref, n2_ref, xs_hbm, zero_buf, sem, pad_sem):
    i = pl.program_id(0)

    @pl.when(i == 0)
    def _():
        zero_buf[...] = jnp.zeros_like(zero_buf)
        for e in range(N_EXPERTS):
            lo = off_ref[e] + cnt_ref[e]
            hi = off_ref[e + 1]

            def pad(r, c):
                _row_copy(zero_buf, 0, xs_hbm, r, pad_sem).start()
                return c

            def pad_wait(r, c):
                _row_copy(zero_buf, 0, xs_hbm, 0, pad_sem).wait()
                return c

            lax.fori_loop(lo, hi, pad, 0)
            lax.fori_loop(lo, hi, pad_wait, 0)

        def tile_copy(t):
            return pltpu.make_async_copy(zero_buf, xs_hbm.at[pl.ds(pl.multiple_of(t * TE, TE), TE)], pad_sem)

        def fill(t, c):
            tile_copy(t).start()
            return c

        def fill_wait(t, c):
            tile_copy(t).wait()
            return c

        first_unused = off_ref[N_EXPERTS] // TE
        lax.fori_loop(first_unused, n_tiles, fill, 0)
        lax.fori_loop(first_unused, n_tiles, fill_wait, 0)

    def issue(r, c):
        _row_copy(n2_ref, r, xs_hbm, pos_ref[0, 0, 2 * r], sem).start()
        _row_copy(n2_ref, r, xs_hbm, pos_ref[0, 0, 2 * r + 1], sem).start(priority=1)
        return c

    def drain(r, c):
        _row_copy(n2_ref, 0, xs_hbm, 0, sem).wait()
        _row_copy(n2_ref, 0, xs_hbm, 0, sem).wait()
        return c

    lax.fori_loop(0, TM, issue, 0, unroll=8)
    lax.fori_loop(0, TM, drain, 0, unroll=8)


def _dispatch(TM, TE, n_tiles, cnt, off, pos, n2):
    N, D = n2.shape
    return pl.pallas_call(
        functools.partial(_dispatch_kernel, TM, TE, n_tiles),
        grid_spec=pltpu.PrefetchScalarGridSpec(
            num_scalar_prefetch=2, grid=(N // TM,),
            in_specs=[pl.BlockSpec((1, 1, 2 * TM), lambda i, c, o: (i, 0, 0), memory_space=pltpu.SMEM),
                      pl.BlockSpec((TM, D), lambda i, c, o: (i, 0))],
            out_specs=pl.BlockSpec(memory_space=pl.ANY),
            scratch_shapes=[pltpu.VMEM((TE, D), F32), pltpu.SemaphoreType.DMA, pltpu.SemaphoreType.DMA]),
        out_shape=jax.ShapeDtypeStruct((n_tiles * TE, D), F32),
        compiler_params=pltpu.CompilerParams(dimension_semantics=("arbitrary",),
                                             disable_bounds_checks=True),
        name=f"moe_dispatch_{TM}",
    )(cnt, off, pos, n2)


def _expert_kernel(te_ref, na_ref, x_ref, wgu_ref, wdn_ref, y_ref):
    used = pl.program_id(0) < na_ref[0]

    @pl.when(used)
    def _():
        gu = _dot(x_ref[...].astype(BF16), wgu_ref[...])
        g = gu[:, :D_FF_EXPERT]
        u = gu[:, D_FF_EXPERT:]
        y_ref[...] = _dot((g * _sigmoid(g) * u).astype(BF16), wdn_ref[...])

    @pl.when(jnp.logical_not(used))
    def _():
        y_ref[...] = jnp.zeros_like(y_ref)


def _experts(layer, TE, te, na, xs, W):
    P, D = xs.shape
    j = layer // 2
    rows = pl.BlockSpec((TE, D), lambda t, te, na: (t, 0))
    return pl.pallas_call(
        _expert_kernel,
        grid_spec=pltpu.PrefetchScalarGridSpec(
            num_scalar_prefetch=2, grid=(P // TE,),
            in_specs=[rows,
                      pl.BlockSpec((None, None, D, 2 * D_FF_EXPERT), lambda t, te, na: (j, te[t], 0, 0)),
                      pl.BlockSpec((None, None, D_FF_EXPERT, D), lambda t, te, na: (j, te[t], 0, 0))],
            out_specs=rows),
        out_shape=jax.ShapeDtypeStruct((P, D), F32),
        compiler_params=pltpu.CompilerParams(dimension_semantics=("arbitrary",),
                                             vmem_limit_bytes=VMEM_LIMIT),
        name=f"moe_experts_{TE}",
    )(te, na, xs, W["moe_w_gu"], W["moe_w_dn"])


def _combine_kernel(TM, pos_ref, h_ref, wts_ref, ys_hbm, gple_ref, wpg_ref, p_ref, wpp_ref,
                    hout_ref, y_buf, sem):
    def issue(r, c):
        _row_copy(ys_hbm, pos_ref[0, 0, 2 * r], y_buf.at[0], r, sem).start()
        _row_copy(ys_hbm, pos_ref[0, 0, 2 * r + 1], y_buf.at[1], r, sem).start(priority=1)
        return c

    def drain(r, c):
        _row_copy(ys_hbm, 0, y_buf.at[0], 0, sem).wait()
        _row_copy(ys_hbm, 0, y_buf.at[1], 0, sem).wait()
        return c

    lax.fori_loop(0, TM, issue, 0, unroll=8)
    lax.fori_loop(0, TM, drain, 0, unroll=8)
    moe = wts_ref[:, 0:1] * y_buf[0] + wts_ref[:, 1:2] * y_buf[1]
    hout_ref[...] = _ple(h_ref[...] + moe, gple_ref, wpg_ref, p_ref, wpp_ref)


def _combine(layer, TM, pos, h, wts, ys, p, W):
    N, D = h.shape
    row = pl.BlockSpec((TM, D), lambda i: (i, 0))
    in_specs = [pl.BlockSpec((1, 1, 2 * TM), lambda i: (i, 0, 0), memory_space=pltpu.SMEM),
                row, pl.BlockSpec((TM, LANES), lambda i: (i, 0)),
                pl.BlockSpec(memory_space=pl.ANY)] + _ple_specs(layer, TM, 1)
    return pl.pallas_call(
        functools.partial(_combine_kernel, TM),
        grid=(N // TM,),
        in_specs=in_specs, out_specs=row,
        out_shape=jax.ShapeDtypeStruct((N, D), F32),
        scratch_shapes=[pltpu.VMEM((2, TM, D), F32), pltpu.SemaphoreType.DMA],
        compiler_params=pltpu.CompilerParams(dimension_semantics=("arbitrary",),
                                             vmem_limit_bytes=VMEM_LIMIT, disable_bounds_checks=True),
        name=f"moe_combine_{TM}",
    )(pos, h, wts, ys, *_ple_args(p, W))


def _ffn_moe(layer, TM, TE, h, p, W):
    N = h.shape[0]
    n_tiles = 2 * N // TE + N_EXPERTS
    n2, idx, wts, cnt = _route(layer, TM, h, W)
    cnt = cnt[0, :N_EXPERTS]
    ends = jnp.cumsum((cnt + TE - 1) // TE).astype(jnp.int32)
    off = jnp.concatenate([jnp.zeros((1,), jnp.int32), ends * TE])
    experts = jnp.arange(N_EXPERTS, dtype=jnp.int32)

    def first_row(e):
        return jnp.sum(jnp.where(e[:, None] == experts[None, :], off[None, :N_EXPERTS], 0), axis=1)

    pos = jnp.stack([first_row(idx[:, 0]) + idx[:, 2], first_row(idx[:, 1]) + idx[:, 3]], axis=1)
    pos = pos.reshape(N // TM, 1, 2 * TM)
    t = jnp.minimum(jnp.arange(n_tiles, dtype=jnp.int32), ends[-1] - 1)
    te = jnp.sum(t[:, None] >= ends[None, :], axis=1).astype(jnp.int32)
    xs = _dispatch(TM, TE, n_tiles, cnt, off, pos, n2)
    ys = _experts(layer, TE, te, ends[-1:], xs, W)
    return _combine(layer, TM, pos, h, wts, ys, p, W)


def _norm_kernel(h_ref, g_ref, o_ref):
    o_ref[...] = _rms(h_ref[...], g_ref[...])


def _final_norm(TM, G, h, g):
    N = h.shape[0]
    row = pl.BlockSpec((TM, D_MODEL), lambda i: (i, 0))
    y = pl.pallas_call(
        _norm_kernel, grid=(N // TM,),
        in_specs=[row, _const_spec((1, D_MODEL))], out_specs=row,
        out_shape=jax.ShapeDtypeStruct((N, D_MODEL), F32),
        compiler_params=pltpu.CompilerParams(dimension_semantics=("parallel",)),
        name=f"final_norm_{TM}",
    )(h, g)
    return y.reshape(N // G, G, D_MODEL).transpose(1, 0, 2)


def _block_diag_gates(wx, wa):
    eye = jnp.eye(HEADS_PER_CHUNK, dtype=wx.dtype)

    def bd(w):
        w = w.reshape(DEPTH, N_GATE_CHUNKS, HEADS_PER_CHUNK, LRU_BLOCK, LRU_BLOCK)
        w = jnp.einsum("dchij,hk->dchikj", w, eye)
        return w.reshape(DEPTH, N_GATE_CHUNKS, GATE_CHUNK, GATE_CHUNK)

    return jnp.concatenate([bd(wx), bd(wa)], axis=-1).astype(BF16)


def _run(h, p, G, TT, ffn_tm, moe_tm, moe_te, lru0, ca0, cb0, W):
    lru_s, ca_s, cb_s = [], [], []
    for layer in range(DEPTH):
        h, lru, ca, cb = _mixer(layer, G, TT, h, ca0[layer], cb0[layer], lru0[layer], W)
        if layer % 2 == 0:
            h = _ffn_dense(layer, ffn_tm, h, p, W)
        else:
            h = _ffn_moe(layer, moe_tm, moe_te, h, p, W)
        lru_s.append(lru)
        ca_s.append(ca)
        cb_s.append(cb)
    y = _final_norm(ffn_tm, G, h, W["final_norm"])
    return y, jnp.stack(lru_s), jnp.stack(ca_s), jnp.stack(cb_s)


def kernel(x_prompt, x_sample, state_lru_h, state_conv_a, state_conv_b, p_prompt, p_sample,
           norm_mix, w_in, conv_a_w, conv_a_b, lru_wx, lru_bx, lru_wa, lru_ba, lru_lambda,
           w_a_out, conv_b_w, w_b_out, w_o, norm_ffn, ffn_w_gu, ffn_w_dn, router,
           moe_w_gu, moe_w_dn, norm_ple, w_ple_gate, w_ple_proj, final_norm):
    B, T, D = x_prompt.shape
    S = x_sample.shape[0]
    KA, KB = CONV_A_WIDTH - 1, CONV_B_WIDTH - 1

    def vec(v):
        return v.reshape(v.shape[0], 1, D)

    W = {
        "norm_mix": vec(norm_mix), "w_in": w_in.astype(BF16),
        "conv_a_w": conv_a_w, "conv_a_b": vec(conv_a_b),
        "w_gate": _block_diag_gates(lru_wx, lru_wa),
        "lru_bx": vec(lru_bx), "lru_ba": vec(lru_ba), "lru_lambda": vec(lru_lambda),
        "w_a_out": w_a_out.astype(BF16), "conv_b_w": conv_b_w, "w_b_out": w_b_out.astype(BF16),
        "w_o": w_o.astype(BF16), "norm_ffn": vec(norm_ffn),
        "ffn_w_gu": ffn_w_gu.astype(BF16), "ffn_w_dn": ffn_w_dn.astype(BF16),
        "router": jnp.pad(router, ((0, 0), (0, 0), (0, LANES - N_EXPERTS))),
        "moe_w_gu": moe_w_gu.astype(BF16), "moe_w_dn": moe_w_dn.astype(BF16),
        "norm_ple": vec(norm_ple), "w_ple_gate": w_ple_gate.astype(BF16),
        "w_ple_proj": w_ple_proj.astype(BF16), "final_norm": final_norm.reshape(1, D),
    }

    hp = x_prompt.transpose(1, 0, 2).reshape(T * B, D)
    pp = p_prompt.transpose(0, 2, 1, 3).reshape(DEPTH, T * B, PLE_DIM).astype(BF16)
    zeros = lambda k: jnp.zeros((DEPTH, k * B, D), F32)
    y_p, lru_p, ca_p, cb_p = _run(hp, pp, B, 64, 1024, 1024, 512, zeros(1), zeros(KA), zeros(KB), W)
    ca_p = ca_p.reshape(DEPTH, KA, B, D).transpose(0, 2, 1, 3)
    cb_p = cb_p.reshape(DEPTH, KB, B, D).transpose(0, 2, 1, 3)

    hs = x_sample.reshape(S, D)
    ps = p_sample.reshape(DEPTH, S, PLE_DIM).astype(BF16)
    ca0 = state_conv_a.transpose(0, 2, 1, 3).reshape(DEPTH, KA * S, D)
    cb0 = state_conv_b.transpose(0, 2, 1, 3).reshape(DEPTH, KB * S, D)
    y_s, lru_d, ca_d, cb_d = _run(hs, ps, S, 1, S, S, S, state_lru_h, ca0, cb0, W)
    ca_d = ca_d.reshape(DEPTH, KA, S, D).transpose(0, 2, 1, 3)
    cb_d = cb_d.reshape(DEPTH, KB, S, D).transpose(0, 2, 1, 3)

    return (y_p, y_s, lru_p, ca_p, cb_p, lru_d, ca_d, cb_d)
```
